```python
import jax, jax.numpy as jnp
from jax import lax
import numpy as np


D_MODEL = 1024
BATCH = 8
SEQ = 4096
DEPTH = 1

MIX_WIDTH = D_MODEL
GLA_WIDTH = MIX_WIDTH // 2
GLA_HEADS = 4
GLA_DV = GLA_WIDTH // GLA_HEADS
GLA_KEY_WIDTH = GLA_WIDTH // 2
GLA_DK = GLA_KEY_WIDTH // GLA_HEADS
GLA_RANK = 16
GLA_TAU = 16.0
MLSTM_WIDTH = MIX_WIDTH - GLA_WIDTH
MLSTM_HEADS = 4
MLSTM_DV = MLSTM_WIDTH // MLSTM_HEADS
MLSTM_QK_WIDTH = MLSTM_WIDTH // 2
MLSTM_DK = MLSTM_QK_WIDTH // MLSTM_HEADS
CONV_WIDTH = 4
CHUNK = 64
D_FF = 2816
ALPHA = (2.0 * DEPTH) ** 0.25
BETA = (8.0 * DEPTH) ** -0.25
LN_EPS = 1e-5
HEAD_NORM_EPS = 1e-6

IN_SIZES = (GLA_KEY_WIDTH, GLA_KEY_WIDTH, GLA_WIDTH, GLA_WIDTH, GLA_RANK,
            MLSTM_WIDTH, MLSTM_WIDTH, MLSTM_WIDTH, MLSTM_HEADS, MLSTM_HEADS)
IN_COLS = sum(IN_SIZES)
IN_SPLIT_POINTS = tuple(int(v) for v in np.cumsum(IN_SIZES)[:-1])

kernel_name = 'hybrid_gla_mlstm_macaron_deepnorm'


def _layer_norm(x, g, b):
    xf = x.astype(jnp.float32)
    mu = jnp.mean(xf, -1, keepdims=True)
    var = jnp.mean(jnp.square(xf - mu), -1, keepdims=True)
    y = (xf - mu) * lax.rsqrt(var + LN_EPS)
    return (y * g.astype(jnp.float32) + b.astype(jnp.float32)).astype(x.dtype)


def _swiglu(x, w_gate, w_up, w_down):
    return (jax.nn.silu(x @ w_gate) * (x @ w_up)) @ w_down


def _head_rms_norm(h, g):
    H, d = h.shape[-2:]
    y = h * lax.rsqrt(jnp.mean(jnp.square(h), -1, keepdims=True) + HEAD_NORM_EPS)
    return y * g.astype(jnp.float32).reshape(H, d)


def _head_layer_norm(h, g):
    H, d = h.shape[-2:]
    mu = jnp.mean(h, -1, keepdims=True)
    var = jnp.mean(jnp.square(h - mu), -1, keepdims=True)
    return (h - mu) * lax.rsqrt(var + HEAD_NORM_EPS) * g.astype(jnp.float32).reshape(H, d)


def _causal_depthwise_conv(u, w, b):
    y = lax.conv_general_dilated(u, w[:, None, :].astype(u.dtype), window_strides=(1,),
                                 padding=[(CONV_WIDTH - 1, 0)],
                                 dimension_numbers=('NWC', 'WIO', 'NWC'),
                                 feature_group_count=u.shape[-1])
    return y + b.astype(u.dtype)


def _to_chunks(t):
    B, S, H, d = t.shape
    return t.reshape(B, S // CHUNK, CHUNK, H, d).transpose(1, 0, 3, 2, 4)


def _gate_to_chunks(t):
    B, S, H = t.shape
    return t.reshape(B, S // CHUNK, CHUNK, H).transpose(1, 0, 3, 2)


def _from_chunks(t):
    N, B, H, L, d = t.shape
    return t.transpose(1, 0, 3, 2, 4).reshape(B, N * L, H, d)


def _gla_chunked(q, k, v, log_a):
    B, S, H, dk = q.shape
    dv = v.shape[-1]
    qc = _to_chunks(q * (dk ** -0.5))
    kc = _to_chunks(k)
    vc = _to_chunks(v)
    bc = jnp.cumsum(_to_chunks(log_a), axis=3)
    mask = jnp.tril(jnp.ones((CHUNK, CHUNK), dtype=bool))

    def step(state, inp):
        q_, k_, v_, b_ = inp
        b_last = b_[:, :, -1:, :]
        b_mid = b_[:, :, CHUNK // 2 - 1:CHUNK // 2, :]
        inter = jnp.einsum('bhld,bhde->bhle', q_ * jnp.exp(b_), state)
        qr = q_ * jnp.exp(b_ - b_mid)
        kr = k_ * jnp.exp(b_mid - b_)
        scores = jnp.where(mask, jnp.einsum('bhld,bhsd->bhls', qr, kr), 0.0)
        intra = jnp.einsum('bhls,bhse->bhle', scores, v_)
        k_dec = k_ * jnp.exp(b_last - b_)
        new_state = (jnp.exp(b_last[:, :, 0, :])[..., None] * state
                     + jnp.einsum('bhld,bhle->bhde', k_dec, v_))
        return new_state, inter + intra

    state0 = jnp.zeros((B, H, dk, dv), jnp.float32)
    _, out = lax.scan(step, state0, (qc, kc, vc, bc))
    return _from_chunks(out)


def _mlstm_chunked(q, k, v, i_pre, log_f):
    B, S, H, dk = q.shape
    dv = v.shape[-1]
    qc = _to_chunks(q * (dk ** -0.5))
    kc = _to_chunks(k)
    vc = _to_chunks(v)
    ic = _gate_to_chunks(i_pre)
    bc = jnp.cumsum(_gate_to_chunks(log_f), axis=-1)
    mask = jnp.tril(jnp.ones((CHUNK, CHUNK), dtype=bool))

    def step(carry, inp):
        C, n, m = carry
        q_, k_, v_, i_, b_ = inp
        b_last = b_[..., -1]
        D = b_[..., :, None] - b_[..., None, :] + i_[..., None, :]
        D = jnp.where(mask, D, -jnp.inf)
        m_inter = b_ + m[..., None]
        m_t = jnp.maximum(m_inter, jnp.max(D, axis=-1))
        w_inter = jnp.exp(m_inter - m_t)
        qk = jnp.einsum('bhtd,bhsd->bhts', q_, k_) * jnp.exp(D - m_t[..., None])
        num = (w_inter[..., None] * jnp.einsum('bhtd,bhde->bhte', q_, C)
               + jnp.einsum('bhts,bhse->bhte', qk, v_))
        den = w_inter * jnp.einsum('bhtd,bhd->bht', q_, n) + jnp.sum(qk, axis=-1)
        h = num / jnp.maximum(jnp.abs(den), jnp.exp(-m_t))[..., None]
        g_end = b_last[..., None] - b_ + i_
        m_new = jnp.maximum(b_last + m, jnp.max(g_end, axis=-1))
        w_state = jnp.exp(b_last + m - m_new)
        w_k = jnp.exp(g_end - m_new[..., None])
        C_new = (w_state[..., None, None] * C
                 + jnp.einsum('bhsd,bhse->bhde', k_ * w_k[..., None], v_))
        n_new = w_state[..., None] * n + jnp.einsum('bhsd,bhs->bhd', k_, w_k)
        return (C_new, n_new, m_new), h

    carry0 = (jnp.zeros((B, H, dk, dv), jnp.float32),
              jnp.zeros((B, H, dk), jnp.float32),
              jnp.zeros((B, H), jnp.float32))
    _, out = lax.scan(step, carry0, (qc, kc, vc, ic, bc))
    return _from_chunks(out)


def _hybrid_mixer(h, w_in, gla_w_lr, gla_b_lr, gla_norm_g, mlstm_conv_w, mlstm_conv_b,
                  mlstm_w_q, mlstm_w_k, mlstm_b_i, mlstm_b_f, mlstm_norm_g, w_out):
    B, S, _ = h.shape
    f32 = jnp.float32
    proj = h @ w_in
    gq, gk, gv, gg, glr, mu, mv, mo, mi, mf = jnp.split(proj, IN_SPLIT_POINTS, axis=-1)

    log_a = jax.nn.log_sigmoid((glr @ gla_w_lr + gla_b_lr).astype(f32)) / GLA_TAU
    o_gla = _gla_chunked(gq.astype(f32).reshape(B, S, GLA_HEADS, GLA_DK),
                         gk.astype(f32).reshape(B, S, GLA_HEADS, GLA_DK),
                         gv.astype(f32).reshape(B, S, GLA_HEADS, GLA_DV),
                         log_a.reshape(B, S, GLA_HEADS, GLA_DK))
    o_gla = _head_rms_norm(o_gla, gla_norm_g).reshape(B, S, GLA_WIDTH) * jax.nn.silu(gg.astype(f32))

    c = jax.nn.silu(_causal_depthwise_conv(mu, mlstm_conv_w, mlstm_conv_b))
    c = c.astype(f32).reshape(B, S, MLSTM_HEADS, MLSTM_DV)
    mq = jnp.einsum('bshc,hcd->bshd', c, mlstm_w_q.astype(f32))
    mk = jnp.einsum('bshc,hcd->bshd', c, mlstm_w_k.astype(f32))
    i_pre = mi.astype(f32) + mlstm_b_i.astype(f32)
    log_f = jax.nn.log_sigmoid(mf.astype(f32) + mlstm_b_f.astype(f32))
    h_m = _mlstm_chunked(mq, mk, mv.astype(f32).reshape(B, S, MLSTM_HEADS, MLSTM_DV), i_pre, log_f)
    o_m = jax.nn.sigmoid(mo.astype(f32)) * _head_layer_norm(h_m, mlstm_norm_g).reshape(B, S, MLSTM_WIDTH)

    y = jnp.concatenate([o_gla, o_m], axis=-1).astype(h.dtype)
    return y @ w_out


def setup_inputs(seed: int = 0) -> dict:
    key = jax.random.key(seed)
    ks = jax.random.split(key, 32)
    L = DEPTH

    def nrm(k, shape, scale):
        return jax.random.normal(k, shape, jnp.float32) * scale

    return {
        'x': nrm(ks[0], (BATCH, SEQ, D_MODEL), 1.0),
        'ln1_g': 1.0 + nrm(ks[1], (L, D_MODEL), 0.02),
        'ln1_b': nrm(ks[2], (L, D_MODEL), 0.02),
        'ffn1_w_gate': nrm(ks[3], (L, D_MODEL, D_FF), D_MODEL ** -0.5),
        'ffn1_w_up': nrm(ks[4], (L, D_MODEL, D_FF), D_MODEL ** -0.5),
        'ffn1_w_down': nrm(ks[5], (L, D_FF, D_MODEL), BETA * D_FF ** -0.5),
        'w_in': nrm(ks[6], (L, D_MODEL, IN_COLS), D_MODEL ** -0.5),
        'gla_w_lr': nrm(ks[7], (L, GLA_RANK, GLA_KEY_WIDTH), GLA_RANK ** -0.5),
        'gla_b_lr': nrm(ks[8], (L, GLA_KEY_WIDTH), 0.02),
        'gla_norm_g': 1.0 + nrm(ks[9], (L, GLA_WIDTH), 0.02),
        'mlstm_conv_w': nrm(ks[10], (L, CONV_WIDTH, MLSTM_WIDTH), CONV_WIDTH ** -0.5),
        'mlstm_conv_b': nrm(ks[11], (L, MLSTM_WIDTH), 0.02),
        'mlstm_w_q': nrm(ks[12], (L, MLSTM_HEADS, MLSTM_DV, MLSTM_DK), MLSTM_DV ** -0.5),
        'mlstm_w_k': nrm(ks[13], (L, MLSTM_HEADS, MLSTM_DV, MLSTM_DK), MLSTM_DV ** -0.5),
        'mlstm_b_i': nrm(ks[14], (L, MLSTM_HEADS), 0.1),
        'mlstm_b_f': jnp.broadcast_to(jnp.linspace(3.0, 6.0, MLSTM_HEADS, dtype=jnp.float32), (L, MLSTM_HEADS))
                     + nrm(ks[15], (L, MLSTM_HEADS), 0.01),
        'mlstm_norm_g': 1.0 + nrm(ks[16], (L, MLSTM_WIDTH), 0.02),
        'w_out': nrm(ks[17], (L, MIX_WIDTH, D_MODEL), BETA * MIX_WIDTH ** -0.5),
        'ln2_g': 1.0 + nrm(ks[18], (L, D_MODEL), 0.02),
        'ln2_b': nrm(ks[19], (L, D_MODEL), 0.02),
        'ffn2_w_gate': nrm(ks[20], (L, D_MODEL, D_FF), D_MODEL ** -0.5),
        'ffn2_w_up': nrm(ks[21], (L, D_MODEL, D_FF), D_MODEL ** -0.5),
        'ffn2_w_down': nrm(ks[22], (L, D_FF, D_MODEL), BETA * D_FF ** -0.5),
        'ln3_g': 1.0 + nrm(ks[23], (L, D_MODEL), 0.02),
        'ln3_b': nrm(ks[24], (L, D_MODEL), 0.02),
    }


def reference(x, ln1_g, ln1_b, ffn1_w_gate, ffn1_w_up, ffn1_w_down, w_in, gla_w_lr, gla_b_lr,
              gla_norm_g, mlstm_conv_w, mlstm_conv_b, mlstm_w_q, mlstm_w_k, mlstm_b_i, mlstm_b_f,
              mlstm_norm_g, w_out, ln2_g, ln2_b, ffn2_w_gate, ffn2_w_up, ffn2_w_down, ln3_g, ln3_b):
    h = x
    for l in range(DEPTH):
        h = _layer_norm(ALPHA * h + 0.5 * _swiglu(h, ffn1_w_gate[l], ffn1_w_up[l], ffn1_w_down[l]),
                        ln1_g[l], ln1_b[l])
        mix = _hybrid_mixer(h, w_in[l], gla_w_lr[l], gla_b_lr[l], gla_norm_g[l], mlstm_conv_w[l],
                            mlstm_conv_b[l], mlstm_w_q[l], mlstm_w_k[l], mlstm_b_i[l], mlstm_b_f[l],
                            mlstm_norm_g[l], w_out[l])
        h = _layer_norm(ALPHA * h + mix, ln2_g[l], ln2_b[l])
        h = _layer_norm(ALPHA * h + 0.5 * _swiglu(h, ffn2_w_gate[l], ffn2_w_up[l], ffn2_w_down[l]),
                        ln3_g[l], ln3_b[l])
    return h
```

```python
import functools

import jax
import jax.numpy as jnp
from jax import lax
from jax.experimental import pallas as pl
from jax.experimental.pallas import tpu as pltpu

F32 = jnp.float32
BF16 = jnp.bfloat16

D_MODEL = 1024
D_FF = 2816
HEADS = 4
DK = 64
DV = 128
KEY_W = HEADS * DK
VAL_W = HEADS * DV
GLA_RANK = 16
GLA_TAU = 16.0
CONV_WIDTH = 4
CHUNK = 64
ALPHA = 2.0 ** 0.25
LN_EPS = 1e-5
HEAD_NORM_EPS = 1e-6
QK_SCALE = DK ** -0.5

Q0 = 0
K0 = Q0 + KEY_W
V0 = K0 + KEY_W
G0 = V0 + VAL_W
MU0 = G0 + VAL_W
MV0 = MU0 + VAL_W
MO0 = MV0 + VAL_W
SM0 = MO0 + VAL_W
SMALL_W = 128
LR_LANE = 0
I_LANE = GLA_RANK
F_LANE = I_LANE + HEADS
PROJ_W = SM0 + SMALL_W

FFN_ROWS = 256
MIX_ROWS = 512
VMEM_LIMIT = 56 * 1024 * 1024


def _layer_norm(z, g, b):
    mu = jnp.mean(z, axis=-1, keepdims=True)
    d = z - mu
    var = jnp.mean(d * d, axis=-1, keepdims=True)
    return d * lax.rsqrt(var + LN_EPS) * g + b


def _silu(x):
    return x * jax.nn.sigmoid(x)


def _dot(a, b):
    return jnp.dot(a, b, preferred_element_type=F32)


def _dot_nt(a, b):
    return lax.dot_general(a, b, (((1,), (1,)), ((), ())), preferred_element_type=F32)


def _dot_tn(a, b):
    return lax.dot_general(a, b, (((0,), (0,)), ((), ())), preferred_element_type=F32)


def _ffn_body(x_ref, wg_ref, wu_ref, wd_ref, lng_ref, lnb_ref, *rest, with_proj):
    if with_proj:
        win_ref, h_ref, proj_ref = rest
    else:
        (h_ref,) = rest
    x = x_ref[...]
    xb = x.astype(BF16)
    g = _dot(xb, wg_ref[...])
    u = _dot(xb, wu_ref[...])
    a = (_silu(g) * u).astype(BF16)
    y = _dot(a, wd_ref[...])
    h = _layer_norm(ALPHA * x + 0.5 * y, lng_ref[...], lnb_ref[...])
    h_ref[...] = h
    if with_proj:
        proj_ref[...] = _dot(h.astype(BF16), win_ref[...])


def _resident(shape):
    return pl.BlockSpec(shape, lambda *_: (0,) * len(shape), pipeline_mode=pl.Buffered(1))


def _ffn_call(x, wg, wu, wd, lng, lnb, win=None):
    t = x.shape[0]
    with_proj = win is not None
    row = lambda i: (i, 0)
    in_specs = [
        pl.BlockSpec((FFN_ROWS, D_MODEL), row),
        _resident((D_MODEL, D_FF)),
        _resident((D_MODEL, D_FF)),
        _resident((D_FF, D_MODEL)),
        _resident((1, D_MODEL)),
        _resident((1, D_MODEL)),
    ]
    args = [x, wg, wu, wd, lng, lnb]
    out_shape = [jax.ShapeDtypeStruct((t, D_MODEL), F32)]
    out_specs = [pl.BlockSpec((FFN_ROWS, D_MODEL), row)]
    if with_proj:
        in_specs.append(_resident((D_MODEL, PROJ_W)))
        args.append(win)
        out_shape.append(jax.ShapeDtypeStruct((t, PROJ_W), F32))
        out_specs.append(pl.BlockSpec((FFN_ROWS, PROJ_W), row))
    return pl.pallas_call(
        functools.partial(_ffn_body, with_proj=with_proj),
        grid=(t // FFN_ROWS,),
        in_specs=in_specs,
        out_specs=out_specs,
        out_shape=out_shape,
        compiler_params=pltpu.CompilerParams(
            dimension_semantics=("arbitrary",), vmem_limit_bytes=VMEM_LIMIT),
        name="ffn_proj" if with_proj else "ffn",
    )(*args)


def _chunk_cumsum(x):
    row_in_chunk = lax.broadcasted_iota(jnp.int32, x.shape, 0) & (CHUNK - 1)
    shift = 1
    while shift < CHUNK:
        x = x + jnp.where(row_in_chunk >= shift, pltpu.roll(x, shift, axis=0), 0.0)
        shift *= 2
    return x


def _mixer_body(proj_ref, h1_ref, wlr_ref, blr_ref, gnorm_ref, convw_ref, convb_ref, wqk_ref,
                gbias_ref, mnorm_ref, wout_ref, lng_ref, lnb_ref, out_ref,
                st_ref, c_ref, m_ref, tail_ref, bc_ref, gcol_ref, bcol_ref, mqk_ref,
                ogla_ref, hm_ref):
    rows_total = proj_ref.shape[0]
    n_chunks = rows_total // CHUNK

    @pl.when(pl.program_id(1) == 0)
    def _():
        st_ref[...] = jnp.zeros_like(st_ref)
        c_ref[...] = jnp.zeros_like(c_ref)
        m_ref[...] = jnp.zeros_like(m_ref)
        tail_ref[...] = jnp.zeros_like(tail_ref)

    small = proj_ref[:, SM0:SM0 + SMALL_W]
    z = _dot(small.astype(BF16), wlr_ref[...]) + blr_ref[...]
    bc_ref[...] = _chunk_cumsum(jax.nn.log_sigmoid(z) * (1.0 / GLA_TAU))

    gates = small + gbias_ref[...]
    gcol_ref[...] = gates
    bcol_ref[...] = _chunk_cumsum(jax.nn.log_sigmoid(gates))

    mu = proj_ref[:, MU0:MU0 + VAL_W]
    tail = tail_ref[...]
    row8 = lax.broadcasted_iota(jnp.int32, (8, VAL_W), 0)
    conv = mu * convw_ref[CONV_WIDTH - 1:CONV_WIDTH, :]
    for lag in range(1, CONV_WIDTH):
        sh = pltpu.roll(mu, lag, axis=0)
        top = jnp.where(row8 < lag, pltpu.roll(tail, lag, axis=0), sh[0:8, :])
        sh = jnp.concatenate([top, sh[8:, :]], axis=0)
        conv = conv + sh * convw_ref[CONV_WIDTH - 1 - lag:CONV_WIDTH - lag, :]
    tail_ref[...] = mu[rows_total - 8:, :]
    cact = _silu(conv + convb_ref[...]).astype(BF16)
    mqk_ref[...] = _dot(cact, wqk_ref[...])

    tril = (lax.broadcasted_iota(jnp.int32, (CHUNK, CHUNK), 0)
            >= lax.broadcasted_iota(jnp.int32, (CHUNK, CHUNK), 1))
    ones_col = (lax.broadcasted_iota(jnp.int32, (CHUNK, DV), 1) == 0).astype(F32)

    def chunk_step(ci, carry):
        r0 = pl.multiple_of(ci * CHUNK, CHUNK)
        rows = pl.ds(r0, CHUNK)

        q = proj_ref[rows, Q0:Q0 + KEY_W] * QK_SCALE
        k = proj_ref[rows, K0:K0 + KEY_W]
        b = bc_ref[rows, :]
        b_last = b[CHUNK - 1:CHUNK, :]
        b_mid = b[CHUNK // 2 - 1:CHUNK // 2, :]
        qe = (q * jnp.exp(b)).astype(BF16)
        qr = (q * jnp.exp(b - b_mid)).astype(BF16)
        kr = (k * jnp.exp(b_mid - b)).astype(BF16)
        kd = (k * jnp.exp(b_last - b)).astype(BF16)
        dec = jnp.exp(b_last)
        for h in range(HEADS):
            ks = slice(h * DK, (h + 1) * DK)
            vs = slice(h * DV, (h + 1) * DV)
            v_h = proj_ref[rows, V0 + h * DV:V0 + (h + 1) * DV].astype(BF16)
            st = st_ref[h]
            inter = _dot_nt(qe[:, ks], st.astype(BF16))
            sc = jnp.where(tril, _dot_nt(qr[:, ks], kr[:, ks]), 0.0)
            intra = _dot(sc.astype(BF16), v_h)
            ogla_ref[rows, vs] = inter + intra
            st_ref[h] = st * dec[:, ks] + _dot_tn(v_h, kd[:, ks])

        gc = gcol_ref[rows, :]
        bm = bcol_ref[rows, :]
        gct = gc.T
        bmt = bm.T
        mq = mqk_ref[rows, 0:KEY_W] * QK_SCALE
        mk = mqk_ref[rows, KEY_W:2 * KEY_W]
        for h in range(HEADS):
            ks = slice(h * DK, (h + 1) * DK)
            vs = slice(h * DV, (h + 1) * DV)
            q_h = mq[:, ks].astype(BF16)
            k_h = mk[:, ks]
            v_h = proj_ref[rows, MV0 + h * DV:MV0 + (h + 1) * DV]
            v_aug = jnp.concatenate([v_h, ones_col], axis=1).astype(BF16)
            b_col = bm[:, F_LANE + h:F_LANE + h + 1]
            i_col = gc[:, I_LANE + h:I_LANE + h + 1]
            b_row = bmt[F_LANE + h:F_LANE + h + 1, :]
            i_row = gct[I_LANE + h:I_LANE + h + 1, :]
            b_end = b_col[CHUNK - 1:CHUNK, :]
            m_prev = m_ref[h][0:1, 0:1]

            dmat = jnp.where(tril, b_col - b_row + i_row, -jnp.inf)
            m_inter = b_col + m_prev
            m_t = jnp.maximum(m_inter, jnp.max(dmat, axis=1, keepdims=True))
            w_inter = jnp.exp(m_inter - m_t)
            qk = (_dot_nt(q_h, k_h.astype(BF16)) * jnp.exp(dmat - m_t)).astype(BF16)
            cst = c_ref[h]
            nd = w_inter * _dot(q_h, cst.astype(BF16)) + _dot(qk, v_aug)
            den = nd[:, DV:DV + 1]
            hm_ref[rows, vs] = nd[:, 0:DV] / jnp.maximum(jnp.abs(den), jnp.exp(-m_t))

            g_end = b_end - b_col + i_col
            m_new = jnp.maximum(b_end + m_prev, jnp.max(g_end, axis=0, keepdims=True))
            w_state = jnp.exp(b_end + m_prev - m_new)
            kw = (k_h * jnp.exp(g_end - m_new)).astype(BF16)
            c_ref[h] = w_state * cst + _dot_tn(kw, v_aug)
            m_ref[h] = jnp.broadcast_to(m_new, m_ref.shape[1:])
        return carry

    lax.fori_loop(0, n_chunks, chunk_step, 0)

    parts = []
    for h in range(HEADS):
        vs = slice(h * DV, (h + 1) * DV)
        o = ogla_ref[:, vs]
        y = o * lax.rsqrt(jnp.mean(o * o, axis=-1, keepdims=True) + HEAD_NORM_EPS)
        y = y * gnorm_ref[:, vs]
        parts.append(y * _silu(proj_ref[:, G0 + h * DV:G0 + (h + 1) * DV]))
    for h in range(HEADS):
        vs = slice(h * DV, (h + 1) * DV)
        x = hm_ref[:, vs]
        d = x - jnp.mean(x, axis=-1, keepdims=True)
        var = jnp.mean(d * d, axis=-1, keepdims=True)
        y = d * lax.rsqrt(var + HEAD_NORM_EPS) * mnorm_ref[:, vs]
        parts.append(jax.nn.sigmoid(proj_ref[:, MO0 + h * DV:MO0 + (h + 1) * DV]) * y)
    y = jnp.concatenate(parts, axis=1).astype(BF16)
    mix = _dot(y, wout_ref[...])
    out_ref[...] = _layer_norm(ALPHA * h1_ref[...] + mix, lng_ref[...], lnb_ref[...])


def _mixer_call(proj, h1, wlr, blr, gnorm, convw, convb, wqk, gbias, mnorm, wout, lng, lnb,
                batch, seq):
    t = proj.shape[0]
    steps = seq // MIX_ROWS
    row = lambda bi, si: (bi * steps + si, 0)
    params = [wlr, blr, gnorm, convw, convb, wqk, gbias, mnorm, wout, lng, lnb]
    in_specs = [pl.BlockSpec((MIX_ROWS, PROJ_W), row), pl.BlockSpec((MIX_ROWS, D_MODEL), row)]
    in_specs += [_resident(p.shape) for p in params]
    scratch = [
        pltpu.VMEM((HEADS, DV, DK), F32),
        pltpu.VMEM((HEADS, DK, 2 * DV), F32),
        pltpu.VMEM((HEADS, 8, 128), F32),
        pltpu.VMEM((8, VAL_W), F32),
        pltpu.VMEM((MIX_ROWS, KEY_W), F32),
        pltpu.VMEM((MIX_ROWS, SMALL_W), F32),
        pltpu.VMEM((MIX_ROWS, SMALL_W), F32),
        pltpu.VMEM((MIX_ROWS, 2 * KEY_W), F32),
        pltpu.VMEM((MIX_ROWS, VAL_W), F32),
        pltpu.VMEM((MIX_ROWS, VAL_W), F32),
    ]
    return pl.pallas_call(
        _mixer_body,
        grid=(batch, steps),
        in_specs=in_specs,
        out_specs=pl.BlockSpec((MIX_ROWS, D_MODEL), row),
        out_shape=jax.ShapeDtypeStruct((t, D_MODEL), F32),
        scratch_shapes=scratch,
        compiler_params=pltpu.CompilerParams(
            dimension_semantics=("arbitrary", "arbitrary"), vmem_limit_bytes=VMEM_LIMIT),
        name="mixer",
    )(proj, h1, *params)


def _block_diag(w):
    h, dv, dk = w.shape
    eye = jnp.eye(h, dtype=w.dtype)
    return (w[:, :, None, :] * eye[:, None, :, None]).reshape(h * dv, h * dk)


def kernel(x, ln1_g, ln1_b, ffn1_w_gate, ffn1_w_up, ffn1_w_down, w_in, gla_w_lr, gla_b_lr,
           gla_norm_g, mlstm_conv_w, mlstm_conv_b, mlstm_w_q, mlstm_w_k, mlstm_b_i, mlstm_b_f,
           mlstm_norm_g, w_out, ln2_g, ln2_b, ffn2_w_gate, ffn2_w_up, ffn2_w_down, ln3_g, ln3_b):
    batch, seq, d = x.shape
    assert d == D_MODEL and seq % MIX_ROWS == 0 and (batch * seq) % FFN_ROWS == 0
    assert ln1_g.shape[0] == 1, "single-layer problem"
    row = lambda v: v.reshape(1, -1).astype(F32)

    wi = w_in[0]
    gq, gk, gv, gg, glr, mu, mv, mo, mi, mf = jnp.split(
        wi, [256, 512, 1024, 1536, 1552, 2064, 2576, 3088, 3092], axis=1)
    small = jnp.concatenate(
        [glr, mi, mf, jnp.zeros((D_MODEL, SMALL_W - GLA_RANK - 2 * HEADS), wi.dtype)], axis=1)
    win = jnp.concatenate([gq, gk, gv, gg, mu, mv, mo, small], axis=1).astype(BF16)

    wlr = jnp.zeros((SMALL_W, KEY_W), F32).at[LR_LANE:LR_LANE + GLA_RANK].set(gla_w_lr[0])
    gbias = (jnp.zeros((1, SMALL_W), F32)
             .at[0, I_LANE:I_LANE + HEADS].set(mlstm_b_i[0])
             .at[0, F_LANE:F_LANE + HEADS].set(mlstm_b_f[0]))
    wqk = jnp.concatenate([_block_diag(mlstm_w_q[0]), _block_diag(mlstm_w_k[0])], axis=1)

    xf = x.reshape(batch * seq, d)
    h1, proj = _ffn_call(xf, ffn1_w_gate[0].astype(BF16), ffn1_w_up[0].astype(BF16),
                         ffn1_w_down[0].astype(BF16), row(ln1_g), row(ln1_b), win)
    h2 = _mixer_call(proj, h1, wlr.astype(BF16), row(gla_b_lr), row(gla_norm_g),
                     mlstm_conv_w[0].astype(F32), row(mlstm_conv_b), wqk.astype(BF16), gbias,
                     row(mlstm_norm_g), w_out[0].astype(BF16), row(ln2_g), row(ln2_b),
                     batch, seq)
    (h3,) = _ffn_call(h2, ffn2_w_gate[0].astype(BF16), ffn2_w_up[0].astype(BF16),
                      ffn2_w_down[0].astype(BF16), row(ln3_g), row(ln3_b))
    return h3.reshape(batch, seq, d)
```

```python
import functools

import jax
import jax.numpy as jnp
from jax import lax
from jax.experimental import pallas as pl
from jax.experimental.pallas import tpu as pltpu

F32 = jnp.float32
BF16 = jnp.bfloat16

D_MODEL = 1024
D_FF = 2816
HEADS = 4
PAIRS = HEADS // 2
DK = 64
DV = 128
KEY_W = HEADS * DK
VAL_W = HEADS * DV
LANES = 128
GLA_RANK = 16
GLA_TAU = 16.0
CONV_WIDTH = 4
CHUNK = 64
ALPHA = 2.0 ** 0.25
LN_EPS = 1e-5
HEAD_NORM_EPS = 1e-6
QK_SCALE = DK ** -0.5

Q0 = 0
K0 = Q0 + KEY_W
V0 = K0 + KEY_W
G0 = V0 + VAL_W
MU0 = G0 + VAL_W
MV0 = MU0 + VAL_W
MO0 = MV0 + VAL_W
SM0 = MO0 + VAL_W
SMALL_W = LANES
LR_LANE = 0
I_LANE = GLA_RANK
F_LANE = I_LANE + HEADS
PROJ_W = SM0 + SMALL_W

FFN_PROJ_ROWS = 256
FFN_ROWS = 512
FFN_SUB = 256
MIX_ROWS = 512
VMEM_LIMIT = 56 * 1024 * 1024


def _layer_norm(z, g, b):
    mu = jnp.mean(z, axis=-1, keepdims=True)
    d = z - mu
    var = jnp.mean(d * d, axis=-1, keepdims=True)
    return d * lax.rsqrt(var + LN_EPS) * g + b


def _sigmoid(x):
    return 0.5 * jnp.tanh(0.5 * x) + 0.5


def _silu(x):
    return x * _sigmoid(x)


def _log_sigmoid(x):
    return jnp.minimum(x, 0.0) - jnp.log1p(jnp.exp(-jnp.abs(x)))


def _dot(a, b):
    return jnp.dot(a, b, preferred_element_type=F32)


def _dot_nt(a, b):
    return lax.dot_general(a, b, (((1,), (1,)), ((), ())), preferred_element_type=F32)


def _dot_tn(a, b):
    return lax.dot_general(a, b, (((0,), (0,)), ((), ())), preferred_element_type=F32)


def _ffn_body(x_ref, wg_ref, wu_ref, wd_ref, lng_ref, lnb_ref, *rest, with_proj, sub_rows):
    if with_proj:
        win_ref, h_ref, proj_ref = rest
    else:
        (h_ref,) = rest
    for s in range(x_ref.shape[0] // sub_rows):
        r = slice(s * sub_rows, (s + 1) * sub_rows)
        x = x_ref[r, :]
        xb = x.astype(BF16)
        g = _dot(xb, wg_ref[...])
        u = _dot(xb, wu_ref[...])
        a = (_silu(g) * u).astype(BF16)
        y = _dot(a, wd_ref[...])
        h = _layer_norm(ALPHA * x + 0.5 * y, lng_ref[...], lnb_ref[...])
        h_ref[r, :] = h
        if with_proj:
            proj_ref[r, :] = _dot(h.astype(BF16), win_ref[...])


def _resident(shape):
    return pl.BlockSpec(shape, lambda *_: (0,) * len(shape), pipeline_mode=pl.Buffered(1))


def _ffn_call(x, wg, wu, wd, lng, lnb, win=None):
    t = x.shape[0]
    with_proj = win is not None
    rows = FFN_PROJ_ROWS if with_proj else FFN_ROWS
    row = lambda i: (i, 0)
    in_specs = [
        pl.BlockSpec((rows, D_MODEL), row),
        _resident((D_MODEL, D_FF)),
        _resident((D_MODEL, D_FF)),
        _resident((D_FF, D_MODEL)),
        _resident((1, D_MODEL)),
        _resident((1, D_MODEL)),
    ]
    args = [x, wg, wu, wd, lng, lnb]
    out_shape = [jax.ShapeDtypeStruct((t, D_MODEL), F32)]
    out_specs = [pl.BlockSpec((rows, D_MODEL), row)]
    if with_proj:
        in_specs.append(_resident((D_MODEL, PROJ_W)))
        args.append(win)
        out_shape.append(jax.ShapeDtypeStruct((t, PROJ_W), F32))
        out_specs.append(pl.BlockSpec((rows, PROJ_W), row))
    return pl.pallas_call(
        functools.partial(_ffn_body, with_proj=with_proj, sub_rows=min(rows, FFN_SUB)),
        grid=(t // rows,),
        in_specs=in_specs,
        out_specs=out_specs,
        out_shape=out_shape,
        compiler_params=pltpu.CompilerParams(
            dimension_semantics=("arbitrary",), vmem_limit_bytes=VMEM_LIMIT),
        name="ffn_proj" if with_proj else "ffn",
    )(*args)


def _chunk_scan(x, combine, fill):
    row_in_chunk = lax.broadcasted_iota(jnp.int32, x.shape, 0) & (CHUNK - 1)
    shift = 1
    while shift < CHUNK:
        x = combine(x, jnp.where(row_in_chunk >= shift, pltpu.roll(x, shift, axis=0), fill))
        shift *= 2
    return x


def _pair_block_diag(x):
    left = lax.broadcasted_iota(jnp.int32, x.shape, 1) < DK
    zero = jnp.zeros_like(x)
    return jnp.concatenate([jnp.where(left, x, zero), jnp.where(left, zero, x)], axis=0)


def _heads_to_lanes(x):
    left = lax.broadcasted_iota(jnp.int32, (x.shape[0], LANES), 1) < DK
    cols = [jnp.broadcast_to(x[:, F_LANE + h:F_LANE + h + 1], (x.shape[0], LANES))
            for h in range(HEADS)]
    return jnp.concatenate([jnp.where(left, cols[0], cols[1]),
                            jnp.where(left, cols[2], cols[3])], axis=1)


def _mixer_body(proj_ref, h1_ref, wlr_ref, blr_ref, gnorm_ref, convw_ref, convb_ref, wqk_ref,
                gbias_ref, mnorm_ref, wout_ref, lng_ref, lnb_ref, out_ref,
                st_ref, c_ref, mw_ref, mg_ref, tail_ref, bc_ref, bcol_ref, acol_ref, amax_ref,
                mqk_ref, ogla_ref, hm_ref):
    rows_total = proj_ref.shape[0]
    n_chunks = rows_total // CHUNK

    @pl.when(pl.program_id(1) == 0)
    def _():
        st_ref[...] = jnp.zeros_like(st_ref)
        c_ref[...] = jnp.zeros_like(c_ref)
        mw_ref[...] = jnp.zeros_like(mw_ref)
        mg_ref[...] = jnp.zeros_like(mg_ref)
        tail_ref[...] = jnp.zeros_like(tail_ref)

    small = proj_ref[:, SM0:SM0 + SMALL_W]
    z = _dot(small.astype(BF16), wlr_ref[...]) + blr_ref[...]
    bc_ref[...] = _chunk_scan(_log_sigmoid(z) * (1.0 / GLA_TAU), jnp.add, 0.0)

    gates = small + gbias_ref[...]
    b_gate = _chunk_scan(_log_sigmoid(gates), jnp.add, 0.0)
    a_gate = pltpu.roll(gates, F_LANE - I_LANE, axis=1) - b_gate
    bcol_ref[...] = b_gate
    acol_ref[...] = a_gate
    amax_ref[...] = _chunk_scan(a_gate, jnp.maximum, -jnp.inf)

    mu = proj_ref[:, MU0:MU0 + VAL_W]
    tail = tail_ref[...]
    row8 = lax.broadcasted_iota(jnp.int32, (8, VAL_W), 0)
    conv = mu * convw_ref[CONV_WIDTH - 1:CONV_WIDTH, :]
    for lag in range(1, CONV_WIDTH):
        sh = pltpu.roll(mu, lag, axis=0)
        top = jnp.where(row8 < lag, pltpu.roll(tail, lag, axis=0), sh[0:8, :])
        sh = jnp.concatenate([top, sh[8:, :]], axis=0)
        conv = conv + sh * convw_ref[CONV_WIDTH - 1 - lag:CONV_WIDTH - lag, :]
    tail_ref[...] = mu[rows_total - 8:, :]
    cact = _silu(conv + convb_ref[...]).astype(BF16)
    mqk_ref[...] = _dot(cact, wqk_ref[...])

    lane_w = lax.broadcasted_iota(jnp.int32, (CHUNK, KEY_W), 1) & (DK - 1)
    t_w = lax.broadcasted_iota(jnp.int32, (CHUNK, KEY_W), 0)
    tril_w = t_w >= lane_w
    diag_w = t_w == lane_w
    zero_v = jnp.zeros((CHUNK, DV), BF16)
    ones_v = jnp.ones((CHUNK, DV), BF16)

    def chunk_step(ci, carry):
        r0 = pl.multiple_of(ci * CHUNK, CHUNK)
        rows = pl.ds(r0, CHUNK)

        q = proj_ref[rows, Q0:Q0 + KEY_W] * QK_SCALE
        k = proj_ref[rows, K0:K0 + KEY_W]
        b = bc_ref[rows, :]
        b_last = b[CHUNK - 1:CHUNK, :]
        b_mid = b[CHUNK // 2 - 1:CHUNK // 2, :]
        qe = (q * jnp.exp(b)).astype(BF16)
        qr = (q * jnp.exp(b - b_mid)).astype(BF16)
        kr = (k * jnp.exp(b_mid - b)).astype(BF16)
        kd = (k * jnp.exp(b_last - b)).astype(BF16)
        dec = jnp.exp(b_last)
        for p in range(PAIRS):
            lp = slice(p * LANES, (p + 1) * LANES)
            va = proj_ref[rows, V0 + 2 * p * DV:V0 + (2 * p + 1) * DV].astype(BF16)
            vb = proj_ref[rows, V0 + (2 * p + 1) * DV:V0 + (2 * p + 2) * DV].astype(BF16)
            sc = _dot_nt(qr[:, lp], _pair_block_diag(kr[:, lp]))
            sc = jnp.where(tril_w[:, lp], sc, 0.0).astype(BF16)
            v_bd = jnp.concatenate([jnp.concatenate([va, zero_v], axis=1),
                                    jnp.concatenate([zero_v, vb], axis=1)], axis=0)
            st = st_ref[p]
            out = _dot_nt(qe[:, lp], _pair_block_diag(st.astype(BF16))) + _dot(sc, v_bd)
            ogla_ref[rows, 2 * p * DV:(2 * p + 2) * DV] = out
            v_stack = jnp.concatenate([va, vb], axis=0)
            st_ref[p] = st * dec[:, lp] + _dot_tn(v_stack, _pair_block_diag(kd[:, lp]))

        b_t = bcol_ref[rows, :]
        a_t = acol_ref[rows, :]
        amax_t = amax_ref[rows, :]
        m_prev_g = mg_ref[0:1, :]
        m_prev_w = mw_ref[0:1, :]
        mt_col = b_t + jnp.maximum(m_prev_g, amax_t)
        a_w = _heads_to_lanes(a_t)
        amax_w = _heads_to_lanes(amax_t)
        big_m = jnp.maximum(m_prev_w, amax_w)
        a_row = jnp.sum(jnp.where(diag_w, a_w, 0.0), axis=0, keepdims=True)
        e_mat = jnp.exp(jnp.where(tril_w, a_row - big_m, -jnp.inf))
        m_end = big_m[CHUNK - 1:CHUNK, :]
        a_end = amax_w[CHUNK - 1:CHUNK, :]
        w_state = jnp.exp(m_prev_w - m_end)
        u_scale = jnp.exp(a_end - m_end)
        mq = mqk_ref[rows, 0:KEY_W] * QK_SCALE
        mk = mqk_ref[rows, KEY_W:2 * KEY_W]
        qb = mq.astype(BF16)
        qw = (mq * jnp.exp(m_prev_w - big_m)).astype(BF16)
        kb = mk.astype(BF16)
        kw = (mk * jnp.exp(a_w - a_end)).astype(BF16)
        for p in range(PAIRS):
            lp = slice(p * LANES, (p + 1) * LANES)
            va = proj_ref[rows, MV0 + 2 * p * DV:MV0 + (2 * p + 1) * DV].astype(BF16)
            vb = proj_ref[rows, MV0 + (2 * p + 1) * DV:MV0 + (2 * p + 2) * DV].astype(BF16)
            pm = (_dot_nt(qb[:, lp], _pair_block_diag(kb[:, lp])) * e_mat[:, lp]).astype(BF16)
            vaug_bd = jnp.concatenate(
                [jnp.concatenate([va, ones_v, zero_v, zero_v], axis=1),
                 jnp.concatenate([zero_v, zero_v, vb, ones_v], axis=1)], axis=0)
            ct = c_ref[p]
            nd = _dot_nt(qw[:, lp], _pair_block_diag(ct.astype(BF16))) + _dot(pm, vaug_bd)
            for j in range(2):
                h = 2 * p + j
                num = nd[:, 2 * j * DV:(2 * j + 1) * DV]
                den = nd[:, (2 * j + 1) * DV:(2 * j + 2) * DV]
                m_t = jnp.broadcast_to(mt_col[:, F_LANE + h:F_LANE + h + 1], (CHUNK, DV))
                hm_ref[rows, h * DV:(h + 1) * DV] = num / jnp.maximum(jnp.abs(den), jnp.exp(-m_t))
            vaug_stack = jnp.concatenate([jnp.concatenate([va, ones_v], axis=1),
                                          jnp.concatenate([vb, ones_v], axis=1)], axis=0)
            upd = _dot_tn(vaug_stack, _pair_block_diag(kw[:, lp]))
            c_ref[p] = ct * w_state[:, lp] + upd * u_scale[:, lp]
        m_new = mt_col[CHUNK - 8:CHUNK, :]
        mg_ref[...] = jnp.broadcast_to(m_new[7:8, :], (8, LANES))
        mw_ref[...] = jnp.broadcast_to(_heads_to_lanes(m_new)[7:8, :], (8, KEY_W))
        return carry

    lax.fori_loop(0, n_chunks, chunk_step, 0, unroll=2)

    parts = []
    for h in range(HEADS):
        vs = slice(h * DV, (h + 1) * DV)
        o = ogla_ref[:, vs]
        y = o * lax.rsqrt(jnp.mean(o * o, axis=-1, keepdims=True) + HEAD_NORM_EPS)
        y = y * gnorm_ref[:, vs]
        parts.append(y * _silu(proj_ref[:, G0 + h * DV:G0 + (h + 1) * DV]))
    for h in range(HEADS):
        vs = slice(h * DV, (h + 1) * DV)
        x = hm_ref[:, vs]
        d = x - jnp.mean(x, axis=-1, keepdims=True)
        var = jnp.mean(d * d, axis=-1, keepdims=True)
        y = d * lax.rsqrt(var + HEAD_NORM_EPS) * mnorm_ref[:, vs]
        parts.append(_sigmoid(proj_ref[:, MO0 + h * DV:MO0 + (h + 1) * DV]) * y)
    y = jnp.concatenate(parts, axis=1).astype(BF16)
    mix = _dot(y, wout_ref[...])
    out_ref[...] = _layer_norm(ALPHA * h1_ref[...] + mix, lng_ref[...], lnb_ref[...])


def _mixer_call(proj, h1, wlr, blr, gnorm, convw, convb, wqk, gbias, mnorm, wout, lng, lnb,
                batch, seq):
    t = proj.shape[0]
    steps = seq // MIX_ROWS
    row = lambda bi, si: (bi * steps + si, 0)
    params = [wlr, blr, gnorm, convw, convb, wqk, gbias, mnorm, wout, lng, lnb]
    in_specs = [pl.BlockSpec((MIX_ROWS, PROJ_W), row), pl.BlockSpec((MIX_ROWS, D_MODEL), row)]
    in_specs += [_resident(p.shape) for p in params]
    scratch = [
        pltpu.VMEM((PAIRS, DV, LANES), F32),
        pltpu.VMEM((PAIRS, 2 * DV, LANES), F32),
        pltpu.VMEM((8, KEY_W), F32),
        pltpu.VMEM((8, LANES), F32),
        pltpu.VMEM((8, VAL_W), F32),
        pltpu.VMEM((MIX_ROWS, KEY_W), F32),
        pltpu.VMEM((MIX_ROWS, SMALL_W), F32),
        pltpu.VMEM((MIX_ROWS, SMALL_W), F32),
        pltpu.VMEM((MIX_ROWS, SMALL_W), F32),
        pltpu.VMEM((MIX_ROWS, 2 * KEY_W), F32),
        pltpu.VMEM((MIX_ROWS, VAL_W), F32),
        pltpu.VMEM((MIX_ROWS, VAL_W), F32),
    ]
    return pl.pallas_call(
        _mixer_body,
        grid=(batch, steps),
        in_specs=in_specs,
        out_specs=pl.BlockSpec((MIX_ROWS, D_MODEL), row),
        out_shape=jax.ShapeDtypeStruct((t, D_MODEL), F32),
        scratch_shapes=scratch,
        compiler_params=pltpu.CompilerParams(
            dimension_semantics=("arbitrary", "arbitrary"), vmem_limit_bytes=VMEM_LIMIT),
        name="mixer",
    )(proj, h1, *params)


def _block_diag(w):
    h, dv, dk = w.shape
    eye = jnp.eye(h, dtype=w.dtype)
    return (w[:, :, None, :] * eye[:, None, :, None]).reshape(h * dv, h * dk)


def kernel(x, ln1_g, ln1_b, ffn1_w_gate, ffn1_w_up, ffn1_w_down, w_in, gla_w_lr, gla_b_lr,
           gla_norm_g, mlstm_conv_w, mlstm_conv_b, mlstm_w_q, mlstm_w_k, mlstm_b_i, mlstm_b_f,
           mlstm_norm_g, w_out, ln2_g, ln2_b, ffn2_w_gate, ffn2_w_up, ffn2_w_down, ln3_g, ln3_b):
    batch, seq, d = x.shape
    assert d == D_MODEL and seq % MIX_ROWS == 0
    assert (batch * seq) % FFN_ROWS == 0 and (batch * seq) % FFN_PROJ_ROWS == 0
    assert ln1_g.shape[0] == 1, "single-layer problem"
    row = lambda v: v.reshape(1, -1).astype(F32)

    wi = w_in[0]
    gq, gk, gv, gg, glr, mu, mv, mo, mi, mf = jnp.split(
        wi, [256, 512, 1024, 1536, 1552, 2064, 2576, 3088, 3092], axis=1)
    small = jnp.concatenate(
        [glr, mi, mf, jnp.zeros((D_MODEL, SMALL_W - GLA_RANK - 2 * HEADS), wi.dtype)], axis=1)
    win = jnp.concatenate([gq, gk, gv, gg, mu, mv, mo, small], axis=1).astype(BF16)

    wlr = jnp.zeros((SMALL_W, KEY_W), F32).at[LR_LANE:LR_LANE + GLA_RANK].set(gla_w_lr[0])
    gbias = (jnp.zeros((1, SMALL_W), F32)
             .at[0, I_LANE:I_LANE + HEADS].set(mlstm_b_i[0])
             .at[0, F_LANE:F_LANE + HEADS].set(mlstm_b_f[0]))
    wqk = jnp.concatenate([_block_diag(mlstm_w_q[0]), _block_diag(mlstm_w_k[0])], axis=1)

    xf = x.reshape(batch * seq, d)
    h1, proj = _ffn_call(xf, ffn1_w_gate[0].astype(BF16), ffn1_w_up[0].astype(BF16),
                         ffn1_w_down[0].astype(BF16), row(ln1_g), row(ln1_b), win)
    h2 = _mixer_call(proj, h1, wlr.astype(BF16), row(gla_b_lr), row(gla_norm_g),
                     mlstm_conv_w[0].astype(F32), row(mlstm_conv_b), wqk.astype(BF16), gbias,
                     row(mlstm_norm_g), w_out[0].astype(BF16), row(ln2_g), row(ln2_b),
                     batch, seq)
    (h3,) = _ffn_call(h2, ffn2_w_gate[0].astype(BF16), ffn2_w_up[0].astype(BF16),
                      ffn2_w_down[0].astype(BF16), row(ln3_g), row(ln3_b))
    return h3.reshape(batch, seq, d)
```

```python
import functools

import jax
import jax.numpy as jnp
from jax import lax
from jax.experimental import pallas as pl
from jax.experimental.pallas import tpu as pltpu

F32 = jnp.float32
BF16 = jnp.bfloat16

D_MODEL = 1024
D_FF = 2816
HEADS = 4
PAIRS = HEADS // 2
DK = 64
DV = 128
KEY_W = HEADS * DK
VAL_W = HEADS * DV
LANES = 128
GLA_RANK = 16
GLA_TAU = 16.0
CONV_WIDTH = 4
CHUNK = 64
ALPHA = 2.0 ** 0.25
LN_EPS = 1e-5
HEAD_NORM_EPS = 1e-6
QK_SCALE = DK ** -0.5

Q0 = 0
K0 = Q0 + KEY_W
V0 = K0 + KEY_W
G0 = V0 + VAL_W
MU0 = G0 + VAL_W
MV0 = MU0 + VAL_W
MO0 = MV0 + VAL_W
SM0 = MO0 + VAL_W
SMALL_W = LANES
LR_LANE = 0
I_LANE = GLA_RANK
F_LANE = I_LANE + HEADS
PROJ_W = SM0 + SMALL_W

FFN_PROJ_ROWS = 512
FFN_SUB = 256
FFN_COLS = 1024
MIX_ROWS = 256
VMEM_LIMIT = 60 * 1024 * 1024


def _layer_norm(z, g, b):
    mu = jnp.mean(z, axis=-1, keepdims=True)
    d = z - mu
    var = jnp.mean(d * d, axis=-1, keepdims=True)
    return d * lax.rsqrt(var + LN_EPS) * g + b


def _sigmoid(x):
    return 0.5 * jnp.tanh(0.5 * x) + 0.5


def _silu(x):
    h = 0.5 * x
    return h * jnp.tanh(h) + h


def _log_sigmoid(x):
    return jnp.minimum(x, 0.0) - jnp.log(1.0 + jnp.exp(-jnp.abs(x)))


def _dot(a, b):
    return jnp.dot(a, b, preferred_element_type=F32)


def _dot_nt(a, b):
    return lax.dot_general(a, b, (((1,), (1,)), ((), ())), preferred_element_type=F32)


def _dot_tn(a, b):
    return lax.dot_general(a, b, (((0,), (0,)), ((), ())), preferred_element_type=F32)


def _ffn_tile(x, wg_ref, wu_ref, wd_ref, lng_ref, lnb_ref):
    xb = x.astype(BF16)
    g = _dot(xb, wg_ref[...])
    u = _dot(xb, wu_ref[...])
    a = (_silu(g) * u).astype(BF16)
    y = _dot(a, wd_ref[...])
    return _layer_norm(ALPHA * x + 0.5 * y, lng_ref[...], lnb_ref[...])


def _ffn_stages(x_ref, slot, wg_ref, wu_ref, wd_ref, lng_ref, lnb_ref, out_ref):
    x = x_ref[slot]
    xb = x.astype(BF16)
    bounds = [min(c, D_FF) for c in range(0, D_FF + FFN_COLS, FFN_COLS)]
    col_chunks = [slice(lo, hi) for lo, hi in zip(bounds[:-1], bounds[1:]) if hi > lo]
    y = None
    gu = None
    for c in range(len(col_chunks) + 1):
        gu_next = None
        if c < len(col_chunks):
            cs = col_chunks[c]
            gu_next = (_dot(xb, wg_ref[:, cs]), _dot(xb, wu_ref[:, cs]))
        if gu is not None:
            part = _dot((_silu(gu[0]) * gu[1]).astype(BF16), wd_ref[col_chunks[c - 1], :])
            y = part if y is None else y + part
        gu = gu_next
        yield
    out_ref[...] = _layer_norm(ALPHA * x_ref[slot] + 0.5 * y, lng_ref[...], lnb_ref[...])


def _run_interleaved(*streams):
    gens = [g for g, _ in streams]
    totals = [n for _, n in streams]
    done = [0] * len(gens)
    live = set(range(len(gens)))
    while live:
        i = min(live, key=lambda j: ((done[j] + 0.5) / totals[j], j))
        try:
            next(gens[i])
            done[i] += 1
        except StopIteration:
            live.remove(i)


def _resident(shape):
    return pl.BlockSpec(shape, lambda *_: (0,) * len(shape), pipeline_mode=pl.Buffered(1))


def _ffn_proj_body(x_ref, wg_ref, wu_ref, wd_ref, lng_ref, lnb_ref, win_ref, h_ref, proj_ref):
    for s in range(x_ref.shape[0] // FFN_SUB):
        r = slice(s * FFN_SUB, (s + 1) * FFN_SUB)
        h = _ffn_tile(x_ref[r, :], wg_ref, wu_ref, wd_ref, lng_ref, lnb_ref)
        h_ref[r, :] = h
        proj_ref[r, :] = _dot(h.astype(BF16), win_ref[...])


def _ffn_proj_call(x, wg, wu, wd, lng, lnb, win):
    t = x.shape[0]
    row = lambda i: (i, 0)
    params = [wg, wu, wd, lng, lnb, win]
    return pl.pallas_call(
        _ffn_proj_body,
        grid=(t // FFN_PROJ_ROWS,),
        in_specs=[pl.BlockSpec((FFN_PROJ_ROWS, D_MODEL), row)] + [_resident(p.shape) for p in params],
        out_specs=[pl.BlockSpec((FFN_PROJ_ROWS, D_MODEL), row),
                   pl.BlockSpec((FFN_PROJ_ROWS, PROJ_W), row)],
        out_shape=[jax.ShapeDtypeStruct((t, D_MODEL), F32),
                   jax.ShapeDtypeStruct((t, PROJ_W), F32)],
        compiler_params=pltpu.CompilerParams(
            dimension_semantics=("arbitrary",), vmem_limit_bytes=VMEM_LIMIT),
        name="ffn_proj",
    )(x, *params)


def _chunk_scan(x, combine, fill):
    row_in_chunk = lax.broadcasted_iota(jnp.int32, x.shape, 0) & (CHUNK - 1)
    shift = 1
    while shift < CHUNK:
        x = combine(x, jnp.where(row_in_chunk >= shift, pltpu.roll(x, shift, axis=0), fill))
        shift *= 2
    return x


def _pair_block_diag(x):
    left = lax.broadcasted_iota(jnp.int32, x.shape, 1) < DK
    zero = jnp.zeros_like(x)
    return jnp.concatenate([jnp.where(left, x, zero), jnp.where(left, zero, x)], axis=0)


def _heads_to_lanes(x):
    left = lax.broadcasted_iota(jnp.int32, (x.shape[0], LANES), 1) < DK
    cols = [jnp.broadcast_to(x[:, F_LANE + h:F_LANE + h + 1], (x.shape[0], LANES))
            for h in range(HEADS)]
    return jnp.concatenate([jnp.where(left, cols[0], cols[1]),
                            jnp.where(left, cols[2], cols[3])], axis=1)


def _mixer_block(proj_ref, h1_ref, wlr_ref, blr_ref, gnorm_ref, convw_ref, convb_ref, wqk_ref,
                 gbias_ref, mnorm_ref, wout_ref, lng_ref, lnb_ref,
                 st_ref, c_ref, mw_ref, mg_ref, ext_ref, bc_ref, bcol_ref, acol_ref, amax_ref,
                 mqk_ref, ogla_ref, hm_ref, h2_ref, slot):
    rows_total = proj_ref.shape[0]
    n_chunks = rows_total // CHUNK

    small = proj_ref[:, SM0:SM0 + SMALL_W]
    z = _dot(small.astype(BF16), wlr_ref[...]) + blr_ref[...]
    bc_ref[...] = _chunk_scan(_log_sigmoid(z) * (1.0 / GLA_TAU), jnp.add, 0.0)
    yield

    gates = small + gbias_ref[...]
    b_gate = _chunk_scan(_log_sigmoid(gates), jnp.add, 0.0)
    a_gate = pltpu.roll(gates, F_LANE - I_LANE, axis=1) - b_gate
    bcol_ref[...] = b_gate
    acol_ref[...] = a_gate
    amax_ref[...] = _chunk_scan(a_gate, jnp.maximum, -jnp.inf)
    yield

    ext_ref[8:, :] = proj_ref[:, MU0:MU0 + VAL_W]
    conv = ext_ref[8:, :] * convw_ref[CONV_WIDTH - 1:CONV_WIDTH, :]
    for lag in range(1, CONV_WIDTH):
        conv = conv + (ext_ref[8 - lag:8 - lag + rows_total, :]
                       * convw_ref[CONV_WIDTH - 1 - lag:CONV_WIDTH - lag, :])
    ext_ref[0:8, :] = ext_ref[rows_total:rows_total + 8, :]
    cact = _silu(conv + convb_ref[...]).astype(BF16)
    mqk_ref[...] = _dot(cact, wqk_ref[...])
    yield

    lane_w = lax.broadcasted_iota(jnp.int32, (CHUNK, KEY_W), 1) & (DK - 1)
    t_w = lax.broadcasted_iota(jnp.int32, (CHUNK, KEY_W), 0)
    tril_w = t_w >= lane_w
    diag_w = t_w == lane_w
    zero_v = jnp.zeros((CHUNK, DV), BF16)
    ones_v = jnp.ones((CHUNK, DV), BF16)

    def chunk_local(ci):
        rows = slice(ci * CHUNK, (ci + 1) * CHUNK)
        loc = {}
        q = proj_ref[rows, Q0:Q0 + KEY_W] * QK_SCALE
        k = proj_ref[rows, K0:K0 + KEY_W]
        b = bc_ref[rows, :]
        b_last = b[CHUNK - 1:CHUNK, :]
        b_mid = b[CHUNK // 2 - 1:CHUNK // 2, :]
        loc["qe"] = (q * jnp.exp(b)).astype(BF16)
        qr = (q * jnp.exp(b - b_mid)).astype(BF16)
        kr = (k * jnp.exp(b_mid - b)).astype(BF16)
        kd = (k * jnp.exp(b_last - b)).astype(BF16)
        loc["dec"] = jnp.exp(b_last)
        loc["gla"] = []
        for p in range(PAIRS):
            lp = slice(p * LANES, (p + 1) * LANES)
            va = proj_ref[rows, V0 + 2 * p * DV:V0 + (2 * p + 1) * DV].astype(BF16)
            vb = proj_ref[rows, V0 + (2 * p + 1) * DV:V0 + (2 * p + 2) * DV].astype(BF16)
            sc = _dot_nt(qr[:, lp], _pair_block_diag(kr[:, lp]))
            v_stack = jnp.concatenate([va, vb], axis=0)
            upd = _dot_tn(v_stack, _pair_block_diag(kd[:, lp]))
            loc["gla"].append((va, vb, sc, upd))
        a_w = _heads_to_lanes(acol_ref[rows, :])
        amax_w = _heads_to_lanes(amax_ref[rows, :])
        a_end = amax_w[CHUNK - 1:CHUNK, :]
        loc["amax_w"] = amax_w
        loc["a_end"] = a_end
        loc["a_row"] = jnp.sum(jnp.where(diag_w, a_w, 0.0), axis=0, keepdims=True)
        mq = mqk_ref[rows, 0:KEY_W] * QK_SCALE
        mk = mqk_ref[rows, KEY_W:2 * KEY_W]
        loc["mq"] = mq
        qb = mq.astype(BF16)
        kb = mk.astype(BF16)
        kw = (mk * jnp.exp(a_w - a_end)).astype(BF16)
        loc["mlstm"] = []
        for p in range(PAIRS):
            lp = slice(p * LANES, (p + 1) * LANES)
            va = proj_ref[rows, MV0 + 2 * p * DV:MV0 + (2 * p + 1) * DV].astype(BF16)
            vb = proj_ref[rows, MV0 + (2 * p + 1) * DV:MV0 + (2 * p + 2) * DV].astype(BF16)
            sqk = _dot_nt(qb[:, lp], _pair_block_diag(kb[:, lp]))
            vaug_stack = jnp.concatenate([jnp.concatenate([va, ones_v], axis=1),
                                          jnp.concatenate([vb, ones_v], axis=1)], axis=0)
            upd = _dot_tn(vaug_stack, _pair_block_diag(kw[:, lp]))
            loc["mlstm"].append((va, vb, sqk, upd))
        return loc

    def chunk_state(ci, loc):
        rows = slice(ci * CHUNK, (ci + 1) * CHUNK)
        for p in range(PAIRS):
            lp = slice(p * LANES, (p + 1) * LANES)
            va, vb, sc, upd = loc["gla"][p]
            sc = jnp.where(tril_w[:, lp], sc, 0.0).astype(BF16)
            v_bd = jnp.concatenate([jnp.concatenate([va, zero_v], axis=1),
                                    jnp.concatenate([zero_v, vb], axis=1)], axis=0)
            st = st_ref[p]
            out = _dot_nt(loc["qe"][:, lp], _pair_block_diag(st.astype(BF16))) + _dot(sc, v_bd)
            ogla_ref[rows, 2 * p * DV:(2 * p + 2) * DV] = out
            st_ref[p] = st * loc["dec"][:, lp] + upd

        m_prev_g = mg_ref[0:1, :]
        m_prev_w = mw_ref[0:1, :]
        mt_col = bcol_ref[rows, :] + jnp.maximum(m_prev_g, amax_ref[rows, :])
        big_m = jnp.maximum(m_prev_w, loc["amax_w"])
        e_mat = jnp.exp(jnp.where(tril_w, loc["a_row"] - big_m, -jnp.inf))
        m_end = big_m[CHUNK - 1:CHUNK, :]
        w_state = jnp.exp(m_prev_w - m_end)
        u_scale = jnp.exp(loc["a_end"] - m_end)
        qw = (loc["mq"] * jnp.exp(m_prev_w - big_m)).astype(BF16)
        for p in range(PAIRS):
            lp = slice(p * LANES, (p + 1) * LANES)
            va, vb, sqk, upd = loc["mlstm"][p]
            pm = (sqk * e_mat[:, lp]).astype(BF16)
            vaug_bd = jnp.concatenate(
                [jnp.concatenate([va, ones_v, zero_v, zero_v], axis=1),
                 jnp.concatenate([zero_v, zero_v, vb, ones_v], axis=1)], axis=0)
            ct = c_ref[p]
            nd = _dot_nt(qw[:, lp], _pair_block_diag(ct.astype(BF16))) + _dot(pm, vaug_bd)
            for j in range(2):
                h = 2 * p + j
                num = nd[:, 2 * j * DV:(2 * j + 1) * DV]
                den = nd[:, (2 * j + 1) * DV:(2 * j + 2) * DV]
                m_t = jnp.broadcast_to(mt_col[:, F_LANE + h:F_LANE + h + 1], (CHUNK, DV))
                hm_ref[rows, h * DV:(h + 1) * DV] = num / jnp.maximum(jnp.abs(den), jnp.exp(-m_t))
            c_ref[p] = ct * w_state[:, lp] + upd * u_scale[:, lp]
        m_new = mt_col[CHUNK - 8:CHUNK, :]
        mg_ref[...] = jnp.broadcast_to(m_new[7:8, :], (8, LANES))
        mw_ref[...] = jnp.broadcast_to(_heads_to_lanes(m_new)[7:8, :], (8, KEY_W))

    pending = None
    for ci in range(n_chunks + 1):
        nxt = chunk_local(ci) if ci < n_chunks else None
        if pending is not None:
            chunk_state(ci - 1, pending)
        pending = nxt
        yield

    parts = []
    for h in range(HEADS):
        vs = slice(h * DV, (h + 1) * DV)
        o = ogla_ref[:, vs]
        y = o * lax.rsqrt(jnp.mean(o * o, axis=-1, keepdims=True) + HEAD_NORM_EPS)
        y = y * gnorm_ref[:, vs]
        parts.append(y * _silu(proj_ref[:, G0 + h * DV:G0 + (h + 1) * DV]))
    yield
    for h in range(HEADS):
        vs = slice(h * DV, (h + 1) * DV)
        x = hm_ref[:, vs]
        d = x - jnp.mean(x, axis=-1, keepdims=True)
        var = jnp.mean(d * d, axis=-1, keepdims=True)
        y = d * lax.rsqrt(var + HEAD_NORM_EPS) * mnorm_ref[:, vs]
        parts.append(_sigmoid(proj_ref[:, MO0 + h * DV:MO0 + (h + 1) * DV]) * y)
    yield
    y = jnp.concatenate(parts, axis=1).astype(BF16)
    mix = _dot(y, wout_ref[...])
    h2_ref[slot] = _layer_norm(ALPHA * h1_ref[...] + mix, lng_ref[...], lnb_ref[...])


N_MIX_PARAMS = 11
N_FFN_PARAMS = 5


def _mix_ffn_body(*refs, blocks_per_seq):
    proj_ref, h1_ref = refs[0:2]
    mix_params = refs[2:2 + N_MIX_PARAMS]
    ffn_params = refs[2 + N_MIX_PARAMS:2 + N_MIX_PARAMS + N_FFN_PARAMS]
    out_ref = refs[2 + N_MIX_PARAMS + N_FFN_PARAMS]
    scratch = refs[3 + N_MIX_PARAMS + N_FFN_PARAMS:]
    hbuf_ref = scratch[0]
    mix_scratch = scratch[1:]
    st_ref, c_ref, mw_ref, mg_ref, ext_ref = mix_scratch[0:5]
    g = pl.program_id(0)
    slot = lax.rem(g, 2)

    @pl.when(g == 0)
    def _():
        hbuf_ref[1] = jnp.zeros(hbuf_ref.shape[1:], F32)

    @pl.when(lax.rem(g, blocks_per_seq) == 0)
    def _():
        st_ref[...] = jnp.zeros_like(st_ref)
        c_ref[...] = jnp.zeros_like(c_ref)
        mw_ref[...] = jnp.zeros_like(mw_ref)
        mg_ref[...] = jnp.zeros_like(mg_ref)
        ext_ref[0:8, :] = jnp.zeros((8, VAL_W), F32)

    n_ffn_stages = -(-D_FF // FFN_COLS) + 2
    n_mix_stages = proj_ref.shape[0] // CHUNK + 7
    _run_interleaved(
        (_mixer_block(proj_ref, h1_ref, *mix_params, *mix_scratch, hbuf_ref, slot), n_mix_stages),
        (_ffn_stages(hbuf_ref, 1 - slot, *ffn_params, out_ref), n_ffn_stages))


def _mix_ffn_call(proj, h1, mix_params, ffn_params, seq):
    t = proj.shape[0]
    n_blocks = t // MIX_ROWS
    cur = lambda g: (jnp.minimum(g, n_blocks - 1), 0)
    prev = lambda g: (jnp.maximum(g - 1, 0), 0)
    assert len(mix_params) == N_MIX_PARAMS and len(ffn_params) == N_FFN_PARAMS
    in_specs = [pl.BlockSpec((MIX_ROWS, PROJ_W), cur), pl.BlockSpec((MIX_ROWS, D_MODEL), cur)]
    in_specs += [_resident(p.shape) for p in (*mix_params, *ffn_params)]
    scratch = [
        pltpu.VMEM((2, MIX_ROWS, D_MODEL), F32),
        pltpu.VMEM((PAIRS, DV, LANES), F32),
        pltpu.VMEM((PAIRS, 2 * DV, LANES), F32),
        pltpu.VMEM((8, KEY_W), F32),
        pltpu.VMEM((8, LANES), F32),
        pltpu.VMEM((MIX_ROWS + 8, VAL_W), F32),
        pltpu.VMEM((MIX_ROWS, KEY_W), F32),
        pltpu.VMEM((MIX_ROWS, SMALL_W), F32),
        pltpu.VMEM((MIX_ROWS, SMALL_W), F32),
        pltpu.VMEM((MIX_ROWS, SMALL_W), F32),
        pltpu.VMEM((MIX_ROWS, 2 * KEY_W), F32),
        pltpu.VMEM((MIX_ROWS, VAL_W), F32),
        pltpu.VMEM((MIX_ROWS, VAL_W), F32),
    ]
    return pl.pallas_call(
        functools.partial(_mix_ffn_body, blocks_per_seq=seq // MIX_ROWS),
        grid=(n_blocks + 1,),
        in_specs=in_specs,
        out_specs=pl.BlockSpec((MIX_ROWS, D_MODEL), prev),
        out_shape=jax.ShapeDtypeStruct((t, D_MODEL), F32),
        scratch_shapes=scratch,
        compiler_params=pltpu.CompilerParams(
            dimension_semantics=("arbitrary",), vmem_limit_bytes=VMEM_LIMIT),
        name="mix_ffn",
    )(proj, h1, *mix_params, *ffn_params)


def _block_diag(w):
    h, dv, dk = w.shape
    eye = jnp.eye(h, dtype=w.dtype)
    return (w[:, :, None, :] * eye[:, None, :, None]).reshape(h * dv, h * dk)


def kernel(x, ln1_g, ln1_b, ffn1_w_gate, ffn1_w_up, ffn1_w_down, w_in, gla_w_lr, gla_b_lr,
           gla_norm_g, mlstm_conv_w, mlstm_conv_b, mlstm_w_q, mlstm_w_k, mlstm_b_i, mlstm_b_f,
           mlstm_norm_g, w_out, ln2_g, ln2_b, ffn2_w_gate, ffn2_w_up, ffn2_w_down, ln3_g, ln3_b):
    batch, seq, d = x.shape
    assert d == D_MODEL and seq % MIX_ROWS == 0 and (batch * seq) % FFN_PROJ_ROWS == 0
    assert ln1_g.shape[0] == 1, "single-layer problem"
    row = lambda v: v.reshape(1, -1).astype(F32)

    wi = w_in[0]
    gq, gk, gv, gg, glr, mu, mv, mo, mi, mf = jnp.split(
        wi, [256, 512, 1024, 1536, 1552, 2064, 2576, 3088, 3092], axis=1)
    small = jnp.concatenate(
        [glr, mi, mf, jnp.zeros((D_MODEL, SMALL_W - GLA_RANK - 2 * HEADS), wi.dtype)], axis=1)
    win = jnp.concatenate([gq, gk, gv, gg, mu, mv, mo, small], axis=1).astype(BF16)

    wlr = jnp.zeros((SMALL_W, KEY_W), F32).at[LR_LANE:LR_LANE + GLA_RANK].set(gla_w_lr[0])
    gbias = (jnp.zeros((1, SMALL_W), F32)
             .at[0, I_LANE:I_LANE + HEADS].set(mlstm_b_i[0])
             .at[0, F_LANE:F_LANE + HEADS].set(mlstm_b_f[0]))
    wqk = jnp.concatenate([_block_diag(mlstm_w_q[0]), _block_diag(mlstm_w_k[0])], axis=1)

    xf = x.reshape(batch * seq, d)
    h1, proj = _ffn_proj_call(xf, ffn1_w_gate[0].astype(BF16), ffn1_w_up[0].astype(BF16),
                              ffn1_w_down[0].astype(BF16), row(ln1_g), row(ln1_b), win)
    mix_params = [wlr.astype(BF16), row(gla_b_lr), row(gla_norm_g), mlstm_conv_w[0].astype(F32),
                  row(mlstm_conv_b), wqk.astype(BF16), gbias, row(mlstm_norm_g),
                  w_out[0].astype(BF16), row(ln2_g), row(ln2_b)]
    ffn_params = [ffn2_w_gate[0].astype(BF16), ffn2_w_up[0].astype(BF16),
                  ffn2_w_down[0].astype(BF16), row(ln3_g), row(ln3_b)]
    h3 = _mix_ffn_call(proj, h1, mix_params, ffn_params, seq)
    return h3.reshape(batch, seq, d)
```

```python
import functools

import jax
import jax.numpy as jnp
from jax import lax
from jax.experimental import pallas as pl
from jax.experimental.pallas import tpu as pltpu

F32 = jnp.float32
BF16 = jnp.bfloat16

D_MODEL = 1024
D_FF = 2816
HEADS = 4
PAIRS = HEADS // 2
DK = 64
DV = 128
KEY_W = HEADS * DK
VAL_W = HEADS * DV
LANES = 128
GLA_RANK = 16
GLA_TAU = 16.0
CONV_WIDTH = 4
CHUNK = 64
ALPHA = 2.0 ** 0.25
LN_EPS = 1e-5
HEAD_NORM_EPS = 1e-6
QK_SCALE = DK ** -0.5

Q0 = 0
K0 = Q0 + KEY_W
V0 = K0 + KEY_W
G0 = V0 + VAL_W
MU0 = G0 + VAL_W
MV0 = MU0 + VAL_W
MO0 = MV0 + VAL_W
SM0 = MO0 + VAL_W
SMALL_W = LANES
LR_LANE = 0
I_LANE = GLA_RANK
F_LANE = I_LANE + HEADS
PROJ_W = SM0 + SMALL_W

FFN_PROJ_ROWS = 512
FFN_SUB = 256
FFN_COLS = 512
PROJ_COLS = 1024
SUB_TILE_LAG = 0.15
MIX_ROWS = 256
CAST_STEPS = 8
VMEM_LIMIT = 60 * 1024 * 1024


def _layer_norm(z, g, b):
    mu = jnp.mean(z, axis=-1, keepdims=True)
    d = z - mu
    var = jnp.mean(d * d, axis=-1, keepdims=True)
    return d * lax.rsqrt(var + LN_EPS) * g + b


def _sigmoid(x):
    return 0.5 * jnp.tanh(0.5 * x) + 0.5


def _silu(x):
    h = 0.5 * x
    return h * jnp.tanh(h) + h


def _log_sigmoid(x):
    return jnp.minimum(x, 0.0) - jnp.log(1.0 + jnp.exp(-jnp.abs(x)))


def _dot(a, b):
    return jnp.dot(a, b, preferred_element_type=F32)


def _dot_nt(a, b):
    return lax.dot_general(a, b, (((1,), (1,)), ((), ())), preferred_element_type=F32)


def _dot_tn(a, b):
    return lax.dot_general(a, b, (((0,), (0,)), ((), ())), preferred_element_type=F32)


def _ffn_tile(x, wg_ref, wu_ref, wd_ref, lng_ref, lnb_ref):
    xb = x.astype(BF16)
    g = _dot(xb, wg_ref[...])
    u = _dot(xb, wu_ref[...])
    a = (_silu(g) * u).astype(BF16)
    y = _dot(a, wd_ref[...])
    return _layer_norm(ALPHA * x + 0.5 * y, lng_ref[...], lnb_ref[...])


def _ffn_col_chunks():
    bounds = [min(c, D_FF) for c in range(0, D_FF + FFN_COLS, FFN_COLS)]
    return [slice(lo, hi) for lo, hi in zip(bounds[:-1], bounds[1:]) if hi > lo]


def _ffn_stages(load_x, wg_ref, wu_ref, wd_ref, lng_ref, lnb_ref, finish):
    xb = load_x().astype(BF16)
    col_chunks = _ffn_col_chunks()
    y = None
    gu = None
    for c in range(len(col_chunks) + 1):
        gu_next = None
        if c < len(col_chunks):
            cs = col_chunks[c]
            gu_next = (_dot(xb, wg_ref[:, cs]), _dot(xb, wu_ref[:, cs]))
        if gu is not None:
            part = _dot((_silu(gu[0]) * gu[1]).astype(BF16), wd_ref[col_chunks[c - 1], :])
            y = part if y is None else y + part
        gu = gu_next
        yield
    yield from finish(_layer_norm(ALPHA * load_x() + 0.5 * y, lng_ref[...], lnb_ref[...]))


def _n_ffn_stages():
    return len(_ffn_col_chunks()) + 1


def _run_interleaved(*streams):
    order = []
    for i, (_, n, start, end) in enumerate(streams):
        order += [(start + (k + 0.5) / n * (end - start), i) for k in range(n)]
    for _, i in sorted(order):
        next(streams[i][0], None)
    for gen, _, _, _ in streams:
        for _ in gen:
            pass


def _resident(shape):
    return pl.BlockSpec(shape, lambda *_: (0,) * len(shape), pipeline_mode=pl.Buffered(1))


def _ffn_proj_body(x_ref, wg_ref, wu_ref, wd_ref, lng_ref, lnb_ref, win_ref, h_ref, proj_ref):
    n_sub = x_ref.shape[0] // FFN_SUB
    proj_chunks = [slice(lo, min(lo + PROJ_COLS, PROJ_W)) for lo in range(0, PROJ_W, PROJ_COLS)]

    def make_stream(s):
        r = slice(s * FFN_SUB, (s + 1) * FFN_SUB)

        def finish(h):
            h_ref[r, :] = h
            hb = h.astype(BF16)
            for cs in proj_chunks:
                proj_ref[r, cs] = _dot(hb, win_ref[:, cs])
                yield

        return _ffn_stages(lambda: x_ref[r, :], wg_ref, wu_ref, wd_ref, lng_ref, lnb_ref, finish)

    n_stages = _n_ffn_stages() + len(proj_chunks)
    lag = SUB_TILE_LAG / max(n_sub - 1, 1)
    _run_interleaved(*[(make_stream(s), n_stages, s * lag, 1.0 - (n_sub - 1 - s) * lag)
                       for s in range(n_sub)])


def _ffn_proj_call(x, wg, wu, wd, lng, lnb, win):
    t = x.shape[0]
    row = lambda i: (i, 0)
    params = [wg, wu, wd, lng, lnb, win]
    return pl.pallas_call(
        _ffn_proj_body,
        grid=(t // FFN_PROJ_ROWS,),
        in_specs=[pl.BlockSpec((FFN_PROJ_ROWS, D_MODEL), row)] + [_resident(p.shape) for p in params],
        out_specs=[pl.BlockSpec((FFN_PROJ_ROWS, D_MODEL), row),
                   pl.BlockSpec((FFN_PROJ_ROWS, PROJ_W), row)],
        out_shape=[jax.ShapeDtypeStruct((t, D_MODEL), F32),
                   jax.ShapeDtypeStruct((t, PROJ_W), F32)],
        compiler_params=pltpu.CompilerParams(
            dimension_semantics=("arbitrary",), vmem_limit_bytes=VMEM_LIMIT),
        name="ffn_proj",
    )(x, *params)


def _chunk_scan(x, combine, fill):
    row_in_chunk = lax.broadcasted_iota(jnp.int32, x.shape, 0) & (CHUNK - 1)
    shift = 1
    while shift < CHUNK:
        x = combine(x, jnp.where(row_in_chunk >= shift, pltpu.roll(x, shift, axis=0), fill))
        shift *= 2
    return x


def _pair_block_diag(x):
    left = lax.broadcasted_iota(jnp.int32, x.shape, 1) < DK
    zero = jnp.zeros_like(x)
    return jnp.concatenate([jnp.where(left, x, zero), jnp.where(left, zero, x)], axis=0)


def _heads_to_lanes(x):
    left = lax.broadcasted_iota(jnp.int32, (x.shape[0], LANES), 1) < DK
    cols = [jnp.broadcast_to(x[:, F_LANE + h:F_LANE + h + 1], (x.shape[0], LANES))
            for h in range(HEADS)]
    return jnp.concatenate([jnp.where(left, cols[0], cols[1]),
                            jnp.where(left, cols[2], cols[3])], axis=1)


def _mixer_block(proj_ref, h1_ref, wlr_ref, blr_ref, gnorm_ref, convw_ref, convb_ref, wqk_ref,
                 gbias_ref, mnorm_ref, wout_ref, lng_ref, lnb_ref,
                 st_ref, c_ref, mw_ref, mg_ref, ext_ref, bc_ref, bcol_ref, acol_ref, amax_ref,
                 mqk_ref, ogla_ref, hm_ref, h2_ref, slot):
    rows_total = proj_ref.shape[0]
    n_chunks = rows_total // CHUNK

    small = proj_ref[:, SM0:SM0 + SMALL_W]
    z = _dot(small.astype(BF16), wlr_ref[...]) + blr_ref[...]
    bc_ref[...] = _chunk_scan(_log_sigmoid(z) * (1.0 / GLA_TAU), jnp.add, 0.0)
    yield

    gates = small + gbias_ref[...]
    b_gate = _chunk_scan(_log_sigmoid(gates), jnp.add, 0.0)
    a_gate = pltpu.roll(gates, F_LANE - I_LANE, axis=1) - b_gate
    bcol_ref[...] = b_gate
    acol_ref[...] = a_gate
    amax_ref[...] = _chunk_scan(a_gate, jnp.maximum, -jnp.inf)
    yield

    ext_ref[8:, :] = proj_ref[:, MU0:MU0 + VAL_W]
    conv = ext_ref[8:, :] * convw_ref[CONV_WIDTH - 1:CONV_WIDTH, :]
    for lag in range(1, CONV_WIDTH):
        conv = conv + (ext_ref[8 - lag:8 - lag + rows_total, :]
                       * convw_ref[CONV_WIDTH - 1 - lag:CONV_WIDTH - lag, :])
    ext_ref[0:8, :] = ext_ref[rows_total:rows_total + 8, :]
    cact = _silu(conv + convb_ref[...]).astype(BF16)
    mqk_ref[...] = _dot(cact, wqk_ref[...])
    yield

    lane_w = lax.broadcasted_iota(jnp.int32, (CHUNK, KEY_W), 1) & (DK - 1)
    t_w = lax.broadcasted_iota(jnp.int32, (CHUNK, KEY_W), 0)
    tril_w = t_w >= lane_w
    diag_w = t_w == lane_w
    zero_v = jnp.zeros((CHUNK, DV), BF16)
    ones_v = jnp.ones((CHUNK, DV), BF16)

    def chunk_local(ci):
        rows = slice(ci * CHUNK, (ci + 1) * CHUNK)
        loc = {}
        q = proj_ref[rows, Q0:Q0 + KEY_W] * QK_SCALE
        k = proj_ref[rows, K0:K0 + KEY_W]
        b = bc_ref[rows, :]
        b_last = b[CHUNK - 1:CHUNK, :]
        b_mid = b[CHUNK // 2 - 1:CHUNK // 2, :]
        loc["qe"] = (q * jnp.exp(b)).astype(BF16)
        qr = (q * jnp.exp(b - b_mid)).astype(BF16)
        kr = (k * jnp.exp(b_mid - b)).astype(BF16)
        kd = (k * jnp.exp(b_last - b)).astype(BF16)
        loc["dec"] = jnp.exp(b_last)
        loc["gla"] = []
        for p in range(PAIRS):
            lp = slice(p * LANES, (p + 1) * LANES)
            va = proj_ref[rows, V0 + 2 * p * DV:V0 + (2 * p + 1) * DV].astype(BF16)
            vb = proj_ref[rows, V0 + (2 * p + 1) * DV:V0 + (2 * p + 2) * DV].astype(BF16)
            sc = _dot_nt(qr[:, lp], _pair_block_diag(kr[:, lp]))
            v_stack = jnp.concatenate([va, vb], axis=0)
            upd = _dot_tn(v_stack, _pair_block_diag(kd[:, lp]))
            loc["gla"].append((va, vb, sc, upd))
        a_w = _heads_to_lanes(acol_ref[rows, :])
        amax_w = _heads_to_lanes(amax_ref[rows, :])
        a_end = amax_w[CHUNK - 1:CHUNK, :]
        loc["amax_w"] = amax_w
        loc["a_end"] = a_end
        loc["a_row"] = jnp.sum(jnp.where(diag_w, a_w, 0.0), axis=0, keepdims=True)
        mq = mqk_ref[rows, 0:KEY_W] * QK_SCALE
        mk = mqk_ref[rows, KEY_W:2 * KEY_W]
        loc["mq"] = mq
        qb = mq.astype(BF16)
        kb = mk.astype(BF16)
        kw = (mk * jnp.exp(a_w - a_end)).astype(BF16)
        loc["mlstm"] = []
        for p in range(PAIRS):
            lp = slice(p * LANES, (p + 1) * LANES)
            va = proj_ref[rows, MV0 + 2 * p * DV:MV0 + (2 * p + 1) * DV].astype(BF16)
            vb = proj_ref[rows, MV0 + (2 * p + 1) * DV:MV0 + (2 * p + 2) * DV].astype(BF16)
            sqk = _dot_nt(qb[:, lp], _pair_block_diag(kb[:, lp]))
            vaug_stack = jnp.concatenate([jnp.concatenate([va, ones_v], axis=1),
                                          jnp.concatenate([vb, ones_v], axis=1)], axis=0)
            upd = _dot_tn(vaug_stack, _pair_block_diag(kw[:, lp]))
            loc["mlstm"].append((va, vb, sqk, upd))
        return loc

    def chunk_state(ci, loc):
        rows = slice(ci * CHUNK, (ci + 1) * CHUNK)
        for p in range(PAIRS):
            lp = slice(p * LANES, (p + 1) * LANES)
            va, vb, sc, upd = loc["gla"][p]
            sc = jnp.where(tril_w[:, lp], sc, 0.0).astype(BF16)
            v_bd = jnp.concatenate([jnp.concatenate([va, zero_v], axis=1),
                                    jnp.concatenate([zero_v, vb], axis=1)], axis=0)
            st = st_ref[p]
            out = _dot_nt(loc["qe"][:, lp], _pair_block_diag(st.astype(BF16))) + _dot(sc, v_bd)
            ogla_ref[rows, 2 * p * DV:(2 * p + 2) * DV] = out
            st_ref[p] = st * loc["dec"][:, lp] + upd

        m_prev_g = mg_ref[0:1, :]
        m_prev_w = mw_ref[0:1, :]
        mt_col = bcol_ref[rows, :] + jnp.maximum(m_prev_g, amax_ref[rows, :])
        big_m = jnp.maximum(m_prev_w, loc["amax_w"])
        e_mat = jnp.exp(jnp.where(tril_w, loc["a_row"] - big_m, -jnp.inf))
        m_end = big_m[CHUNK - 1:CHUNK, :]
        w_state = jnp.exp(m_prev_w - m_end)
        u_scale = jnp.exp(loc["a_end"] - m_end)
        qw = (loc["mq"] * jnp.exp(m_prev_w - big_m)).astype(BF16)
        for p in range(PAIRS):
            lp = slice(p * LANES, (p + 1) * LANES)
            va, vb, sqk, upd = loc["mlstm"][p]
            pm = (sqk * e_mat[:, lp]).astype(BF16)
            vaug_bd = jnp.concatenate(
                [jnp.concatenate([va, ones_v, zero_v, zero_v], axis=1),
                 jnp.concatenate([zero_v, zero_v, vb, ones_v], axis=1)], axis=0)
            ct = c_ref[p]
            nd = _dot_nt(qw[:, lp], _pair_block_diag(ct.astype(BF16))) + _dot(pm, vaug_bd)
            for j in range(2):
                h = 2 * p + j
                num = nd[:, 2 * j * DV:(2 * j + 1) * DV]
                den = nd[:, (2 * j + 1) * DV:(2 * j + 2) * DV]
                m_t = jnp.broadcast_to(mt_col[:, F_LANE + h:F_LANE + h + 1], (CHUNK, DV))
                hm_ref[rows, h * DV:(h + 1) * DV] = num / jnp.maximum(jnp.abs(den), jnp.exp(-m_t))
            c_ref[p] = ct * w_state[:, lp] + upd * u_scale[:, lp]
        m_new = mt_col[CHUNK - 8:CHUNK, :]
        mg_ref[...] = jnp.broadcast_to(m_new[7:8, :], (8, LANES))
        mw_ref[...] = jnp.broadcast_to(_heads_to_lanes(m_new)[7:8, :], (8, KEY_W))

    pending = None
    for ci in range(n_chunks + 1):
        nxt = chunk_local(ci) if ci < n_chunks else None
        if pending is not None:
            chunk_state(ci - 1, pending)
        pending = nxt
        yield

    parts = []
    for h in range(HEADS):
        vs = slice(h * DV, (h + 1) * DV)
        o = ogla_ref[:, vs]
        y = o * lax.rsqrt(jnp.mean(o * o, axis=-1, keepdims=True) + HEAD_NORM_EPS)
        y = y * gnorm_ref[:, vs]
        parts.append(y * _silu(proj_ref[:, G0 + h * DV:G0 + (h + 1) * DV]))
    yield
    for h in range(HEADS):
        vs = slice(h * DV, (h + 1) * DV)
        x = hm_ref[:, vs]
        d = x - jnp.mean(x, axis=-1, keepdims=True)
        var = jnp.mean(d * d, axis=-1, keepdims=True)
        y = d * lax.rsqrt(var + HEAD_NORM_EPS) * mnorm_ref[:, vs]
        parts.append(_sigmoid(proj_ref[:, MO0 + h * DV:MO0 + (h + 1) * DV]) * y)
    yield
    y = jnp.concatenate(parts, axis=1).astype(BF16)
    mix = _dot(y, wout_ref[...])
    h2_ref[slot] = _layer_norm(ALPHA * h1_ref[...] + mix, lng_ref[...], lnb_ref[...])


N_MIX_PARAMS = 11
N_FFN_PARAMS = 5


def _mix_ffn_body(*refs, blocks_per_seq):
    proj_ref, h1_ref = refs[0:2]
    mix_params = refs[2:2 + N_MIX_PARAMS]
    ffn_params = refs[2 + N_MIX_PARAMS:2 + N_MIX_PARAMS + N_FFN_PARAMS]
    out_ref = refs[2 + N_MIX_PARAMS + N_FFN_PARAMS]
    scratch = refs[3 + N_MIX_PARAMS + N_FFN_PARAMS:]
    hbuf_ref = scratch[0]
    mix_scratch = scratch[1:]
    st_ref, c_ref, mw_ref, mg_ref, ext_ref = mix_scratch[0:5]
    g = pl.program_id(0)
    slot = lax.rem(g, 2)

    @pl.when(g == 0)
    def _():
        hbuf_ref[1] = jnp.zeros(hbuf_ref.shape[1:], F32)

    @pl.when(lax.rem(g, blocks_per_seq) == 0)
    def _():
        st_ref[...] = jnp.zeros_like(st_ref)
        c_ref[...] = jnp.zeros_like(c_ref)
        mw_ref[...] = jnp.zeros_like(mw_ref)
        mg_ref[...] = jnp.zeros_like(mg_ref)
        ext_ref[0:8, :] = jnp.zeros((8, VAL_W), F32)

    def store_out(h):
        out_ref[...] = h
        yield

    n_mix_stages = proj_ref.shape[0] // CHUNK + 7
    _run_interleaved(
        (_mixer_block(proj_ref, h1_ref, *mix_params, *mix_scratch, hbuf_ref, slot),
         n_mix_stages, 0.0, 1.0),
        (_ffn_stages(lambda: hbuf_ref[1 - slot], *ffn_params, store_out),
         _n_ffn_stages() + 1, 0.0, 1.0))


def _mix_ffn_call(proj, h1, mix_params, ffn_params, seq):
    t = proj.shape[0]
    n_blocks = t // MIX_ROWS
    cur = lambda g: (jnp.minimum(g, n_blocks - 1), 0)
    prev = lambda g: (jnp.maximum(g - 1, 0), 0)
    assert len(mix_params) == N_MIX_PARAMS and len(ffn_params) == N_FFN_PARAMS
    in_specs = [pl.BlockSpec((MIX_ROWS, PROJ_W), cur), pl.BlockSpec((MIX_ROWS, D_MODEL), cur)]
    in_specs += [_resident(p.shape) for p in (*mix_params, *ffn_params)]
    scratch = [
        pltpu.VMEM((2, MIX_ROWS, D_MODEL), F32),
        pltpu.VMEM((PAIRS, DV, LANES), F32),
        pltpu.VMEM((PAIRS, 2 * DV, LANES), F32),
        pltpu.VMEM((8, KEY_W), F32),
        pltpu.VMEM((8, LANES), F32),
        pltpu.VMEM((MIX_ROWS + 8, VAL_W), F32),
        pltpu.VMEM((MIX_ROWS, KEY_W), F32),
        pltpu.VMEM((MIX_ROWS, SMALL_W), F32),
        pltpu.VMEM((MIX_ROWS, SMALL_W), F32),
        pltpu.VMEM((MIX_ROWS, SMALL_W), F32),
        pltpu.VMEM((MIX_ROWS, 2 * KEY_W), F32),
        pltpu.VMEM((MIX_ROWS, VAL_W), F32),
        pltpu.VMEM((MIX_ROWS, VAL_W), F32),
    ]
    return pl.pallas_call(
        functools.partial(_mix_ffn_body, blocks_per_seq=seq // MIX_ROWS),
        grid=(n_blocks + 1,),
        in_specs=in_specs,
        out_specs=pl.BlockSpec((MIX_ROWS, D_MODEL), prev),
        out_shape=jax.ShapeDtypeStruct((t, D_MODEL), F32),
        scratch_shapes=scratch,
        compiler_params=pltpu.CompilerParams(
            dimension_semantics=("arbitrary",), vmem_limit_bytes=VMEM_LIMIT),
        name="mix_ffn",
    )(proj, h1, *mix_params, *ffn_params)


def _cast_body(*refs):
    n = len(refs) // 2
    for src, dst in zip(refs[:n], refs[n:]):
        dst[...] = src[...].astype(dst.dtype)


def _to_bf16(*arrays):
    def spec(a):
        assert a.ndim == 2 and a.shape[0] % (16 * CAST_STEPS) == 0
        return pl.BlockSpec((a.shape[0] // CAST_STEPS, a.shape[1]), lambda i: (i, 0))
    return pl.pallas_call(
        _cast_body,
        grid=(CAST_STEPS,),
        in_specs=[spec(a) for a in arrays],
        out_specs=[spec(a) for a in arrays],
        out_shape=[jax.ShapeDtypeStruct(a.shape, BF16) for a in arrays],
        compiler_params=pltpu.CompilerParams(
            dimension_semantics=("arbitrary",), vmem_limit_bytes=VMEM_LIMIT),
        name="weights_to_bf16",
    )(*arrays)


def _block_diag(w):
    h, dv, dk = w.shape
    eye = jnp.eye(h, dtype=w.dtype)
    return (w[:, :, None, :] * eye[:, None, :, None]).reshape(h * dv, h * dk)


def kernel(x, ln1_g, ln1_b, ffn1_w_gate, ffn1_w_up, ffn1_w_down, w_in, gla_w_lr, gla_b_lr,
           gla_norm_g, mlstm_conv_w, mlstm_conv_b, mlstm_w_q, mlstm_w_k, mlstm_b_i, mlstm_b_f,
           mlstm_norm_g, w_out, ln2_g, ln2_b, ffn2_w_gate, ffn2_w_up, ffn2_w_down, ln3_g, ln3_b):
    batch, seq, d = x.shape
    assert d == D_MODEL and seq % MIX_ROWS == 0 and (batch * seq) % FFN_PROJ_ROWS == 0
    assert ln1_g.shape[0] == 1, "single-layer problem"
    row = lambda v: v.reshape(1, -1).astype(F32)

    w1g, w1u, w1d, w2g, w2u, w2d, wi, wo = _to_bf16(
        ffn1_w_gate[0], ffn1_w_up[0], ffn1_w_down[0], ffn2_w_gate[0], ffn2_w_up[0],
        ffn2_w_down[0], w_in[0], w_out[0])

    gq, gk, gv, gg, glr, mu, mv, mo, mi, mf = jnp.split(
        wi, [256, 512, 1024, 1536, 1552, 2064, 2576, 3088, 3092], axis=1)
    small = jnp.concatenate(
        [glr, mi, mf, jnp.zeros((D_MODEL, SMALL_W - GLA_RANK - 2 * HEADS), wi.dtype)], axis=1)
    win = jnp.concatenate([gq, gk, gv, gg, mu, mv, mo, small], axis=1)

    wlr = jnp.zeros((SMALL_W, KEY_W), F32).at[LR_LANE:LR_LANE + GLA_RANK].set(gla_w_lr[0])
    gbias = (jnp.zeros((1, SMALL_W), F32)
             .at[0, I_LANE:I_LANE + HEADS].set(mlstm_b_i[0])
             .at[0, F_LANE:F_LANE + HEADS].set(mlstm_b_f[0]))
    wqk = jnp.concatenate([_block_diag(mlstm_w_q[0]), _block_diag(mlstm_w_k[0])], axis=1)

    xf = x.reshape(batch * seq, d)
    h1, proj = _ffn_proj_call(xf, w1g, w1u, w1d, row(ln1_g), row(ln1_b), win)
    mix_params = [wlr.astype(BF16), row(gla_b_lr), row(gla_norm_g), mlstm_conv_w[0].astype(F32),
                  row(mlstm_conv_b), wqk.astype(BF16), gbias, row(mlstm_norm_g),
                  wo, row(ln2_g), row(ln2_b)]
    ffn_params = [w2g, w2u, w2d, row(ln3_g), row(ln3_b)]
    h3 = _mix_ffn_call(proj, h1, mix_params, ffn_params, seq)
    return h3.reshape(batch, seq, d)
```

```python
import functools

import jax
import jax.numpy as jnp
from jax import lax
from jax.experimental import pallas as pl
from jax.experimental.pallas import tpu as pltpu

F32 = jnp.float32
BF16 = jnp.bfloat16

D_MODEL = 1024
D_FF = 2816
HEADS = 4
PAIRS = HEADS // 2
DK = 64
DV = 128
KEY_W = HEADS * DK
VAL_W = HEADS * DV
LANES = 128
GLA_RANK = 16
GLA_TAU = 16.0
CONV_WIDTH = 4
CHUNK = 64
ALPHA = 2.0 ** 0.25
LN_EPS = 1e-5
HEAD_NORM_EPS = 1e-6
QK_SCALE = DK ** -0.5

Q0 = 0
K0 = Q0 + KEY_W
V0 = K0 + KEY_W
G0 = V0 + VAL_W
MU0 = G0 + VAL_W
MV0 = MU0 + VAL_W
MO0 = MV0 + VAL_W
SM0 = MO0 + VAL_W
SMALL_W = LANES
LR_LANE = 0
I_LANE = GLA_RANK
F_LANE = I_LANE + HEADS
PROJ_W = SM0 + SMALL_W

FFN_PROJ_ROWS = 512
FFN_SUB = 256
FFN_COLS = 512
PROJ_COLS = 1024
SUB_TILE_LAG = 0.15
MIX_ROWS = 512
CAST_STEPS = 8
VMEM_LIMIT = 60 * 1024 * 1024


def _layer_norm(z, g, b):
    mu = jnp.mean(z, axis=-1, keepdims=True)
    d = z - mu
    var = jnp.mean(d * d, axis=-1, keepdims=True)
    return d * lax.rsqrt(var + LN_EPS) * g + b


def _sigmoid(x):
    return 0.5 * jnp.tanh(0.5 * x) + 0.5


def _silu(x):
    h = 0.5 * x
    return h * jnp.tanh(h) + h


def _log_sigmoid(x):
    return jnp.minimum(x, 0.0) - jnp.log(1.0 + jnp.exp(-jnp.abs(x)))


def _dot(a, b):
    return jnp.dot(a, b, preferred_element_type=F32)


def _dot_nt(a, b):
    return lax.dot_general(a, b, (((1,), (1,)), ((), ())), preferred_element_type=F32)


def _dot_tn(a, b):
    return lax.dot_general(a, b, (((0,), (0,)), ((), ())), preferred_element_type=F32)


def _ffn_tile(x, wg_ref, wu_ref, wd_ref, lng_ref, lnb_ref):
    xb = x.astype(BF16)
    g = _dot(xb, wg_ref[...])
    u = _dot(xb, wu_ref[...])
    a = (_silu(g) * u).astype(BF16)
    y = _dot(a, wd_ref[...])
    return _layer_norm(ALPHA * x + 0.5 * y, lng_ref[...], lnb_ref[...])


def _ffn_col_chunks():
    bounds = [min(c, D_FF) for c in range(0, D_FF + FFN_COLS, FFN_COLS)]
    return [slice(lo, hi) for lo, hi in zip(bounds[:-1], bounds[1:]) if hi > lo]


def _ffn_stages(load_x, wg_ref, wu_ref, wd_ref, lng_ref, lnb_ref, finish):
    xb = load_x().astype(BF16)
    col_chunks = _ffn_col_chunks()
    y = None
    gu = None
    for c in range(len(col_chunks) + 1):
        gu_next = None
        if c < len(col_chunks):
            cs = col_chunks[c]
            gu_next = (_dot(xb, wg_ref[:, cs]), _dot(xb, wu_ref[:, cs]))
        if gu is not None:
            part = _dot((_silu(gu[0]) * gu[1]).astype(BF16), wd_ref[col_chunks[c - 1], :])
            y = part if y is None else y + part
        gu = gu_next
        yield
    yield from finish(_layer_norm(ALPHA * load_x() + 0.5 * y, lng_ref[...], lnb_ref[...]))


def _n_ffn_stages():
    return len(_ffn_col_chunks()) + 1


def _run_interleaved(*streams):
    order = []
    for i, (_, n, start, end) in enumerate(streams):
        order += [(start + (k + 0.5) / n * (end - start), i) for k in range(n)]
    for _, i in sorted(order):
        next(streams[i][0], None)
    for gen, _, _, _ in streams:
        for _ in gen:
            pass


def _resident(shape):
    return pl.BlockSpec(shape, lambda *_: (0,) * len(shape), pipeline_mode=pl.Buffered(1))


def _ffn_proj_body(x_ref, wg_ref, wu_ref, wd_ref, lng_ref, lnb_ref, win_ref, h_ref, proj_ref):
    n_sub = x_ref.shape[0] // FFN_SUB
    proj_chunks = [slice(lo, min(lo + PROJ_COLS, PROJ_W)) for lo in range(0, PROJ_W, PROJ_COLS)]

    def make_stream(s):
        r = slice(s * FFN_SUB, (s + 1) * FFN_SUB)

        def finish(h):
            h_ref[r, :] = h
            hb = h.astype(BF16)
            for cs in proj_chunks:
                proj_ref[r, cs] = _dot(hb, win_ref[:, cs])
                yield

        return _ffn_stages(lambda: x_ref[r, :], wg_ref, wu_ref, wd_ref, lng_ref, lnb_ref, finish)

    n_stages = _n_ffn_stages() + len(proj_chunks)
    lag = SUB_TILE_LAG / max(n_sub - 1, 1)
    _run_interleaved(*[(make_stream(s), n_stages, s * lag, 1.0 - (n_sub - 1 - s) * lag)
                       for s in range(n_sub)])


def _ffn_proj_call(x, wg, wu, wd, lng, lnb, win):
    t = x.shape[0]
    row = lambda i: (i, 0)
    params = [wg, wu, wd, lng, lnb, win]
    return pl.pallas_call(
        _ffn_proj_body,
        grid=(t // FFN_PROJ_ROWS,),
        in_specs=[pl.BlockSpec((FFN_PROJ_ROWS, D_MODEL), row)] + [_resident(p.shape) for p in params],
        out_specs=[pl.BlockSpec((FFN_PROJ_ROWS, D_MODEL), row),
                   pl.BlockSpec((FFN_PROJ_ROWS, PROJ_W), row)],
        out_shape=[jax.ShapeDtypeStruct((t, D_MODEL), F32),
                   jax.ShapeDtypeStruct((t, PROJ_W), F32)],
        compiler_params=pltpu.CompilerParams(
            dimension_semantics=("arbitrary",), vmem_limit_bytes=VMEM_LIMIT),
        name="ffn_proj",
    )(x, *params)


def _chunk_scan(x, combine, fill):
    row_in_chunk = lax.broadcasted_iota(jnp.int32, x.shape, 0) & (CHUNK - 1)
    shift = 1
    while shift < CHUNK:
        x = combine(x, jnp.where(row_in_chunk >= shift, pltpu.roll(x, shift, axis=0), fill))
        shift *= 2
    return x


def _pair_block_diag(x):
    left = lax.broadcasted_iota(jnp.int32, x.shape, 1) < DK
    zero = jnp.zeros_like(x)
    return jnp.concatenate([jnp.where(left, x, zero), jnp.where(left, zero, x)], axis=0)


def _heads_to_lanes(x):
    left = lax.broadcasted_iota(jnp.int32, (x.shape[0], LANES), 1) < DK
    cols = [jnp.broadcast_to(x[:, F_LANE + h:F_LANE + h + 1], (x.shape[0], LANES))
            for h in range(HEADS)]
    return jnp.concatenate([jnp.where(left, cols[0], cols[1]),
                            jnp.where(left, cols[2], cols[3])], axis=1)


def _mixer_block(proj_ref, h1_ref, wlr_ref, blr_ref, gnorm_ref, convw_ref, convb_ref, wqk_ref,
                 gbias_ref, mnorm_ref, wout_ref, lng_ref, lnb_ref,
                 st_ref, c_ref, mw_ref, mg_ref, ext_ref, bc_ref, bcol_ref, acol_ref, amax_ref,
                 mqk_ref, ogla_ref, hm_ref, h2_ref, slot):
    rows_total = proj_ref.shape[0]
    n_chunks = rows_total // CHUNK

    small = proj_ref[:, SM0:SM0 + SMALL_W]
    z = _dot(small.astype(BF16), wlr_ref[...]) + blr_ref[...]
    bc_ref[...] = _chunk_scan(_log_sigmoid(z) * (1.0 / GLA_TAU), jnp.add, 0.0)
    yield

    gates = small + gbias_ref[...]
    b_gate = _chunk_scan(_log_sigmoid(gates), jnp.add, 0.0)
    a_gate = pltpu.roll(gates, F_LANE - I_LANE, axis=1) - b_gate
    bcol_ref[...] = b_gate
    acol_ref[...] = a_gate
    amax_ref[...] = _chunk_scan(a_gate, jnp.maximum, -jnp.inf)
    yield

    ext_ref[8:, :] = proj_ref[:, MU0:MU0 + VAL_W]
    conv = ext_ref[8:, :] * convw_ref[CONV_WIDTH - 1:CONV_WIDTH, :]
    for lag in range(1, CONV_WIDTH):
        conv = conv + (ext_ref[8 - lag:8 - lag + rows_total, :]
                       * convw_ref[CONV_WIDTH - 1 - lag:CONV_WIDTH - lag, :])
    ext_ref[0:8, :] = ext_ref[rows_total:rows_total + 8, :]
    cact = _silu(conv + convb_ref[...]).astype(BF16)
    mqk_ref[...] = _dot(cact, wqk_ref[...])
    yield

    lane_w = lax.broadcasted_iota(jnp.int32, (CHUNK, KEY_W), 1) & (DK - 1)
    t_w = lax.broadcasted_iota(jnp.int32, (CHUNK, KEY_W), 0)
    tril_w = t_w >= lane_w
    diag_w = t_w == lane_w
    zero_v = jnp.zeros((CHUNK, DV), BF16)
    ones_v = jnp.ones((CHUNK, DV), BF16)

    def chunk_local(ci):
        rows = slice(ci * CHUNK, (ci + 1) * CHUNK)
        loc = {}
        q = proj_ref[rows, Q0:Q0 + KEY_W] * QK_SCALE
        k = proj_ref[rows, K0:K0 + KEY_W]
        b = bc_ref[rows, :]
        b_last = b[CHUNK - 1:CHUNK, :]
        b_mid = b[CHUNK // 2 - 1:CHUNK // 2, :]
        loc["qe"] = (q * jnp.exp(b)).astype(BF16)
        qr = (q * jnp.exp(b - b_mid)).astype(BF16)
        kr = (k * jnp.exp(b_mid - b)).astype(BF16)
        kd = (k * jnp.exp(b_last - b)).astype(BF16)
        loc["dec"] = jnp.exp(b_last)
        loc["gla"] = []
        for p in range(PAIRS):
            lp = slice(p * LANES, (p + 1) * LANES)
            va = proj_ref[rows, V0 + 2 * p * DV:V0 + (2 * p + 1) * DV].astype(BF16)
            vb = proj_ref[rows, V0 + (2 * p + 1) * DV:V0 + (2 * p + 2) * DV].astype(BF16)
            sc = _dot_nt(qr[:, lp], _pair_block_diag(kr[:, lp]))
            v_stack = jnp.concatenate([va, vb], axis=0)
            upd = _dot_tn(v_stack, _pair_block_diag(kd[:, lp]))
            loc["gla"].append((va, vb, sc, upd))
        a_w = _heads_to_lanes(acol_ref[rows, :])
        amax_w = _heads_to_lanes(amax_ref[rows, :])
        a_end = amax_w[CHUNK - 1:CHUNK, :]
        loc["amax_w"] = amax_w
        loc["a_end"] = a_end
        loc["a_row"] = jnp.sum(jnp.where(diag_w, a_w, 0.0), axis=0, keepdims=True)
        mq = mqk_ref[rows, 0:KEY_W] * QK_SCALE
        mk = mqk_ref[rows, KEY_W:2 * KEY_W]
        loc["mq"] = mq
        qb = mq.astype(BF16)
        kb = mk.astype(BF16)
        kw = (mk * jnp.exp(a_w - a_end)).astype(BF16)
        loc["mlstm"] = []
        for p in range(PAIRS):
            lp = slice(p * LANES, (p + 1) * LANES)
            va = proj_ref[rows, MV0 + 2 * p * DV:MV0 + (2 * p + 1) * DV].astype(BF16)
            vb = proj_ref[rows, MV0 + (2 * p + 1) * DV:MV0 + (2 * p + 2) * DV].astype(BF16)
            sqk = _dot_nt(qb[:, lp], _pair_block_diag(kb[:, lp]))
            vaug_stack = jnp.concatenate([jnp.concatenate([va, ones_v], axis=1),
                                          jnp.concatenate([vb, ones_v], axis=1)], axis=0)
            upd = _dot_tn(vaug_stack, _pair_block_diag(kw[:, lp]))
            loc["mlstm"].append((va, vb, sqk, upd))
        return loc

    def chunk_state(ci, loc):
        rows = slice(ci * CHUNK, (ci + 1) * CHUNK)
        for p in range(PAIRS):
            lp = slice(p * LANES, (p + 1) * LANES)
            va, vb, sc, upd = loc["gla"][p]
            sc = jnp.where(tril_w[:, lp], sc, 0.0).astype(BF16)
            v_bd = jnp.concatenate([jnp.concatenate([va, zero_v], axis=1),
                                    jnp.concatenate([zero_v, vb], axis=1)], axis=0)
            st = st_ref[p]
            out = _dot_nt(loc["qe"][:, lp], _pair_block_diag(st.astype(BF16))) + _dot(sc, v_bd)
            ogla_ref[rows, 2 * p * DV:(2 * p + 2) * DV] = out
            st_ref[p] = st * loc["dec"][:, lp] + upd

        m_prev_g = mg_ref[0:1, :]
        m_prev_w = mw_ref[0:1, :]
        mt_col = bcol_ref[rows, :] + jnp.maximum(m_prev_g, amax_ref[rows, :])
        big_m = jnp.maximum(m_prev_w, loc["amax_w"])
        e_mat = jnp.exp(jnp.where(tril_w, loc["a_row"] - big_m, -jnp.inf))
        m_end = big_m[CHUNK - 1:CHUNK, :]
        w_state = jnp.exp(m_prev_w - m_end)
        u_scale = jnp.exp(loc["a_end"] - m_end)
        qw = (loc["mq"] * jnp.exp(m_prev_w - big_m)).astype(BF16)
        for p in range(PAIRS):
            lp = slice(p * LANES, (p + 1) * LANES)
            va, vb, sqk, upd = loc["mlstm"][p]
            pm = (sqk * e_mat[:, lp]).astype(BF16)
            vaug_bd = jnp.concatenate(
                [jnp.concatenate([va, ones_v, zero_v, zero_v], axis=1),
                 jnp.concatenate([zero_v, zero_v, vb, ones_v], axis=1)], axis=0)
            ct = c_ref[p]
            nd = _dot_nt(qw[:, lp], _pair_block_diag(ct.astype(BF16))) + _dot(pm, vaug_bd)
            for j in range(2):
                h = 2 * p + j
                num = nd[:, 2 * j * DV:(2 * j + 1) * DV]
                den = nd[:, (2 * j + 1) * DV:(2 * j + 2) * DV]
                m_t = jnp.broadcast_to(mt_col[:, F_LANE + h:F_LANE + h + 1], (CHUNK, DV))
                hm_ref[rows, h * DV:(h + 1) * DV] = num / jnp.maximum(jnp.abs(den), jnp.exp(-m_t))
            c_ref[p] = ct * w_state[:, lp] + upd * u_scale[:, lp]
        m_new = mt_col[CHUNK - 8:CHUNK, :]
        mg_ref[...] = jnp.broadcast_to(m_new[7:8, :], (8, LANES))
        mw_ref[...] = jnp.broadcast_to(_heads_to_lanes(m_new)[7:8, :], (8, KEY_W))

    pending = None
    for ci in range(n_chunks + 1):
        nxt = chunk_local(ci) if ci < n_chunks else None
        if pending is not None:
            chunk_state(ci - 1, pending)
        pending = nxt
        yield

    parts = []
    for h in range(HEADS):
        vs = slice(h * DV, (h + 1) * DV)
        o = ogla_ref[:, vs]
        y = o * lax.rsqrt(jnp.mean(o * o, axis=-1, keepdims=True) + HEAD_NORM_EPS)
        y = y * gnorm_ref[:, vs]
        parts.append(y * _silu(proj_ref[:, G0 + h * DV:G0 + (h + 1) * DV]))
    yield
    for h in range(HEADS):
        vs = slice(h * DV, (h + 1) * DV)
        x = hm_ref[:, vs]
        d = x - jnp.mean(x, axis=-1, keepdims=True)
        var = jnp.mean(d * d, axis=-1, keepdims=True)
        y = d * lax.rsqrt(var + HEAD_NORM_EPS) * mnorm_ref[:, vs]
        parts.append(_sigmoid(proj_ref[:, MO0 + h * DV:MO0 + (h + 1) * DV]) * y)
    yield
    y = jnp.concatenate(parts, axis=1).astype(BF16)
    mix = _dot(y, wout_ref[...])
    h2_ref[slot] = _layer_norm(ALPHA * h1_ref[...] + mix, lng_ref[...], lnb_ref[...])


N_MIX_PARAMS = 11
N_FFN_PARAMS = 5


def _mix_ffn_body(*refs, blocks_per_seq):
    proj_ref, h1_ref = refs[0:2]
    mix_params = refs[2:2 + N_MIX_PARAMS]
    ffn_params = refs[2 + N_MIX_PARAMS:2 + N_MIX_PARAMS + N_FFN_PARAMS]
    out_ref = refs[2 + N_MIX_PARAMS + N_FFN_PARAMS]
    scratch = refs[3 + N_MIX_PARAMS + N_FFN_PARAMS:]
    hbuf_ref = scratch[0]
    mix_scratch = scratch[1:]
    st_ref, c_ref, mw_ref, mg_ref, ext_ref = mix_scratch[0:5]
    g = pl.program_id(0)
    slot = lax.rem(g, 2)

    @pl.when(g == 0)
    def _():
        hbuf_ref[1] = jnp.zeros(hbuf_ref.shape[1:], F32)

    @pl.when(lax.rem(g, blocks_per_seq) == 0)
    def _():
        st_ref[...] = jnp.zeros_like(st_ref)
        c_ref[...] = jnp.zeros_like(c_ref)
        mw_ref[...] = jnp.zeros_like(mw_ref)
        mg_ref[...] = jnp.zeros_like(mg_ref)
        ext_ref[0:8, :] = jnp.zeros((8, VAL_W), F32)

    def store_out(h):
        out_ref[...] = h
        yield

    n_mix_stages = proj_ref.shape[0] // CHUNK + 7
    _run_interleaved(
        (_mixer_block(proj_ref, h1_ref, *mix_params, *mix_scratch, hbuf_ref, slot),
         n_mix_stages, 0.0, 1.0),
        (_ffn_stages(lambda: hbuf_ref[1 - slot], *ffn_params, store_out),
         _n_ffn_stages() + 1, 0.0, 1.0))


def _mix_ffn_call(proj, h1, mix_params, ffn_params, seq):
    t = proj.shape[0]
    n_blocks = t // MIX_ROWS
    cur = lambda g: (jnp.minimum(g, n_blocks - 1), 0)
    prev = lambda g: (jnp.maximum(g - 1, 0), 0)
    assert len(mix_params) == N_MIX_PARAMS and len(ffn_params) == N_FFN_PARAMS
    in_specs = [pl.BlockSpec((MIX_ROWS, PROJ_W), cur), pl.BlockSpec((MIX_ROWS, D_MODEL), cur)]
    in_specs += [_resident(p.shape) for p in (*mix_params, *ffn_params)]
    scratch = [
        pltpu.VMEM((2, MIX_ROWS, D_MODEL), F32),
        pltpu.VMEM((PAIRS, DV, LANES), F32),
        pltpu.VMEM((PAIRS, 2 * DV, LANES), F32),
        pltpu.VMEM((8, KEY_W), F32),
        pltpu.VMEM((8, LANES), F32),
        pltpu.VMEM((MIX_ROWS + 8, VAL_W), F32),
        pltpu.VMEM((MIX_ROWS, KEY_W), F32),
        pltpu.VMEM((MIX_ROWS, SMALL_W), F32),
        pltpu.VMEM((MIX_ROWS, SMALL_W), F32),
        pltpu.VMEM((MIX_ROWS, SMALL_W), F32),
        pltpu.VMEM((MIX_ROWS, 2 * KEY_W), F32),
        pltpu.VMEM((MIX_ROWS, VAL_W), F32),
        pltpu.VMEM((MIX_ROWS, VAL_W), F32),
    ]
    return pl.pallas_call(
        functools.partial(_mix_ffn_body, blocks_per_seq=seq // MIX_ROWS),
        grid=(n_blocks + 1,),
        in_specs=in_specs,
        out_specs=pl.BlockSpec((MIX_ROWS, D_MODEL), prev),
        out_shape=jax.ShapeDtypeStruct((t, D_MODEL), F32),
        scratch_shapes=scratch,
        compiler_params=pltpu.CompilerParams(
            dimension_semantics=("arbitrary",), vmem_limit_bytes=VMEM_LIMIT),
        name="mix_ffn",
    )(proj, h1, *mix_params, *ffn_params)


def _cast_body(*refs):
    n_plain = (len(refs) - 3) // 2
    srcs, (win_src, small_src) = refs[:n_plain], refs[n_plain:n_plain + 2]
    dsts, win_dst = refs[n_plain + 2:-1], refs[-1]
    for src, dst in zip(srcs, dsts):
        dst[...] = src[...].astype(dst.dtype)
    head_w = G0 + VAL_W
    body_w = SM0 - MU0
    win_dst[:, 0:head_w] = win_src[:, 0:head_w].astype(BF16)
    win_dst[:, MU0:SM0] = win_src[:, head_w + GLA_RANK:head_w + GLA_RANK + body_w].astype(BF16)
    win_dst[:, SM0:PROJ_W] = small_src[...].astype(BF16)


def _round_weights(plain, w_in, small):
    def spec(rows, cols):
        assert rows % (16 * CAST_STEPS) == 0
        return pl.BlockSpec((rows // CAST_STEPS, cols), lambda i: (i, 0))
    rows = w_in.shape[0]
    return pl.pallas_call(
        _cast_body,
        grid=(CAST_STEPS,),
        in_specs=[spec(*a.shape) for a in (*plain, w_in, small)],
        out_specs=[spec(*a.shape) for a in plain] + [spec(rows, PROJ_W)],
        out_shape=[jax.ShapeDtypeStruct(a.shape, BF16) for a in plain]
        + [jax.ShapeDtypeStruct((rows, PROJ_W), BF16)],
        compiler_params=pltpu.CompilerParams(
            dimension_semantics=("arbitrary",), vmem_limit_bytes=VMEM_LIMIT),
        name="round_weights",
    )(*plain, w_in, small)


def _block_diag(w):
    h, dv, dk = w.shape
    eye = jnp.eye(h, dtype=w.dtype)
    return (w[:, :, None, :] * eye[:, None, :, None]).reshape(h * dv, h * dk)


def kernel(x, ln1_g, ln1_b, ffn1_w_gate, ffn1_w_up, ffn1_w_down, w_in, gla_w_lr, gla_b_lr,
           gla_norm_g, mlstm_conv_w, mlstm_conv_b, mlstm_w_q, mlstm_w_k, mlstm_b_i, mlstm_b_f,
           mlstm_norm_g, w_out, ln2_g, ln2_b, ffn2_w_gate, ffn2_w_up, ffn2_w_down, ln3_g, ln3_b):
    batch, seq, d = x.shape
    assert d == D_MODEL and seq % MIX_ROWS == 0 and (batch * seq) % FFN_PROJ_ROWS == 0
    assert ln1_g.shape[0] == 1, "single-layer problem"
    row = lambda v: v.reshape(1, -1).astype(F32)

    wi = w_in[0]
    lr0 = G0 + VAL_W
    gate0 = wi.shape[1] - 2 * HEADS
    small = jnp.concatenate(
        [wi[:, lr0:lr0 + GLA_RANK], wi[:, gate0:],
         jnp.zeros((D_MODEL, SMALL_W - GLA_RANK - 2 * HEADS), wi.dtype)], axis=1)
    w1g, w1u, w1d, w2g, w2u, w2d, wo, win = _round_weights(
        [ffn1_w_gate[0], ffn1_w_up[0], ffn1_w_down[0], ffn2_w_gate[0], ffn2_w_up[0],
         ffn2_w_down[0], w_out[0]], wi, small)

    wlr = jnp.zeros((SMALL_W, KEY_W), F32).at[LR_LANE:LR_LANE + GLA_RANK].set(gla_w_lr[0])
    gbias = (jnp.zeros((1, SMALL_W), F32)
             .at[0, I_LANE:I_LANE + HEADS].set(mlstm_b_i[0])
             .at[0, F_LANE:F_LANE + HEADS].set(mlstm_b_f[0]))
    wqk = jnp.concatenate([_block_diag(mlstm_w_q[0]), _block_diag(mlstm_w_k[0])], axis=1)

    xf = x.reshape(batch * seq, d)
    h1, proj = _ffn_proj_call(xf, w1g, w1u, w1d, row(ln1_g), row(ln1_b), win)
    mix_params = [wlr.astype(BF16), row(gla_b_lr), row(gla_norm_g), mlstm_conv_w[0].astype(F32),
                  row(mlstm_conv_b), wqk.astype(BF16), gbias, row(mlstm_norm_g),
                  wo, row(ln2_g), row(ln2_b)]
    ffn_params = [w2g, w2u, w2d, row(ln3_g), row(ln3_b)]
    h3 = _mix_ffn_call(proj, h1, mix_params, ffn_params, seq)
    return h3.reshape(batch, seq, d)
```

```python
import functools

import jax
import jax.numpy as jnp
from jax import lax
from jax.experimental import pallas as pl
from jax.experimental.pallas import tpu as pltpu

F32 = jnp.float32
BF16 = jnp.bfloat16

D_MODEL = 1024
D_FF = 2816
HEADS = 4
PAIRS = HEADS // 2
DK = 64
DV = 128
KEY_W = HEADS * DK
VAL_W = HEADS * DV
LANES = 128
GLA_RANK = 16
GLA_TAU = 16.0
CONV_WIDTH = 4
CHUNK = 64
ALPHA = 2.0 ** 0.25
LN_EPS = 1e-5
HEAD_NORM_EPS = 1e-6
QK_SCALE = DK ** -0.5

Q0 = 0
K0 = Q0 + KEY_W
V0 = K0 + KEY_W
G0 = V0 + VAL_W
MU0 = G0 + VAL_W
MV0 = MU0 + VAL_W
MO0 = MV0 + VAL_W
SM0 = MO0 + VAL_W
SMALL_W = LANES
LR_LANE = 0
I_LANE = GLA_RANK
F_LANE = I_LANE + HEADS
PROJ_W = SM0 + SMALL_W

FFN_PROJ_ROWS = 512
FFN_SUB = 256
FFN_COLS = 512
PROJ_COLS = 1024
SUB_TILE_LAG = 0.15
MIX_ROWS = 256
LN_ROWS = 64
PRE_ROWS = 64
PIPE_DEPTH = 3
CAST_STEPS = 8
VMEM_LIMIT = 60 * 1024 * 1024


def _layer_norm(z, g, b):
    mu = jnp.mean(z, axis=-1, keepdims=True)
    d = z - mu
    var = jnp.mean(d * d, axis=-1, keepdims=True)
    return d * lax.rsqrt(var + LN_EPS) * g + b


def _sigmoid(x):
    return 0.5 * jnp.tanh(0.5 * x) + 0.5


def _silu(x):
    h = 0.5 * x
    return h * jnp.tanh(h) + h


def _log_sigmoid(x):
    return jnp.minimum(x, 0.0) - jnp.log(1.0 + jnp.exp(-jnp.abs(x)))


def _dot(a, b):
    return jnp.dot(a, b, preferred_element_type=F32)


def _dot_nt(a, b):
    return lax.dot_general(a, b, (((1,), (1,)), ((), ())), preferred_element_type=F32)


def _dot_tn(a, b):
    return lax.dot_general(a, b, (((0,), (0,)), ((), ())), preferred_element_type=F32)


def _ffn_col_chunks():
    bounds = [min(c, D_FF) for c in range(0, D_FF + FFN_COLS, FFN_COLS)]
    return [slice(lo, hi) for lo, hi in zip(bounds[:-1], bounds[1:]) if hi > lo]


def _ffn_stages(load_x, wg_ref, wu_ref, wd_ref, finish):
    xb = load_x().astype(BF16)
    col_chunks = _ffn_col_chunks()
    y = None
    gu = None
    for c in range(len(col_chunks) + 1):
        gu_next = None
        if c < len(col_chunks):
            cs = col_chunks[c]
            gate = _dot(xb, wg_ref[:, cs])
            yield
            gu_next = (gate, _dot(xb, wu_ref[:, cs]))
            yield
        if gu is not None:
            part = _dot((_silu(gu[0]) * gu[1]).astype(BF16), wd_ref[col_chunks[c - 1], :])
            y = part if y is None else y + part
            yield
        gu = gu_next
    yield from finish(ALPHA * load_x() + 0.5 * y)


def _n_ffn_stages():
    return 3 * len(_ffn_col_chunks())


def _row_groups(rows):
    return [slice(lo, lo + LN_ROWS) for lo in range(0, rows, LN_ROWS)]


def _run_interleaved(*streams):
    order = []
    for i, (_, n, start, end) in enumerate(streams):
        order += [(start + (k + 0.5) / n * (end - start), i) for k in range(n)]
    for _, i in sorted(order):
        next(streams[i][0], None)
    for gen, _, _, _ in streams:
        for _ in gen:
            pass


def _resident(shape):
    return pl.BlockSpec(shape, lambda *_: (0,) * len(shape), pipeline_mode=pl.Buffered(1))


def _ffn_proj_body(x_ref, wg_ref, wu_ref, wd_ref, lng_ref, lnb_ref, win_ref, h_ref, proj_ref):
    n_sub = x_ref.shape[0] // FFN_SUB
    proj_chunks = [slice(lo, min(lo + PROJ_COLS, PROJ_W)) for lo in range(0, PROJ_W, PROJ_COLS)]

    def make_stream(s):
        r = slice(s * FFN_SUB, (s + 1) * FFN_SUB)

        def finish(z):
            pieces = []
            for g in _row_groups(FFN_SUB):
                h = _layer_norm(z[g, :], lng_ref[...], lnb_ref[...])
                h_ref[s * FFN_SUB + g.start:s * FFN_SUB + g.stop, :] = h
                pieces.append(h.astype(BF16))
                yield
            hb = jnp.concatenate(pieces, axis=0)
            for cs in proj_chunks:
                proj_ref[r, cs] = _dot(hb, win_ref[:, cs])
                yield

        return _ffn_stages(lambda: x_ref[r, :], wg_ref, wu_ref, wd_ref, finish)

    n_stages = _n_ffn_stages() + len(_row_groups(FFN_SUB)) + len(proj_chunks)
    lag = SUB_TILE_LAG / max(n_sub - 1, 1)
    _run_interleaved(*[(make_stream(s), n_stages, s * lag, 1.0 - (n_sub - 1 - s) * lag)
                       for s in range(n_sub)])


def _ffn_proj_call(x, wg, wu, wd, lng, lnb, win):
    t = x.shape[0]
    row = lambda i: (i, 0)
    params = [wg, wu, wd, lng, lnb, win]
    return pl.pallas_call(
        _ffn_proj_body,
        grid=(t // FFN_PROJ_ROWS,),
        in_specs=[pl.BlockSpec((FFN_PROJ_ROWS, D_MODEL), row)] + [_resident(p.shape) for p in params],
        out_specs=[pl.BlockSpec((FFN_PROJ_ROWS, D_MODEL), row),
                   pl.BlockSpec((FFN_PROJ_ROWS, PROJ_W), row)],
        out_shape=[jax.ShapeDtypeStruct((t, D_MODEL), F32),
                   jax.ShapeDtypeStruct((t, PROJ_W), F32)],
        compiler_params=pltpu.CompilerParams(
            dimension_semantics=("arbitrary",), vmem_limit_bytes=VMEM_LIMIT),
        name="ffn_proj",
    )(x, *params)


def _chunk_scan(x, combine, fill):
    row_in_chunk = lax.broadcasted_iota(jnp.int32, x.shape, 0) & (CHUNK - 1)
    shift = 1
    while shift < CHUNK:
        x = combine(x, jnp.where(row_in_chunk >= shift, pltpu.roll(x, shift, axis=0), fill))
        shift *= 2
    return x


def _pair_block_diag(x):
    left = lax.broadcasted_iota(jnp.int32, x.shape, 1) < DK
    zero = jnp.zeros_like(x)
    return jnp.concatenate([jnp.where(left, x, zero), jnp.where(left, zero, x)], axis=0)


def _heads_to_lanes(x):
    left = lax.broadcasted_iota(jnp.int32, (x.shape[0], LANES), 1) < DK
    cols = [jnp.broadcast_to(x[:, F_LANE + h:F_LANE + h + 1], (x.shape[0], LANES))
            for h in range(HEADS)]
    return jnp.concatenate([jnp.where(left, cols[0], cols[1]),
                            jnp.where(left, cols[2], cols[3])], axis=1)


def _mixer_epilogue(gg_ref, mo_ref, h1_ref, gnorm_ref, mnorm_ref, wout_ref, lng_ref, lnb_ref,
                    ogla_ref, hm_ref, h2_ref, slot):
    parts = []
    for h in range(HEADS):
        vs = slice(h * DV, (h + 1) * DV)
        o = ogla_ref[:, vs]
        y = o * lax.rsqrt(jnp.mean(o * o, axis=-1, keepdims=True) + HEAD_NORM_EPS)
        y = y * gnorm_ref[:, vs]
        parts.append((y * _silu(gg_ref[:, vs])).astype(BF16))
        yield
    for h in range(HEADS):
        vs = slice(h * DV, (h + 1) * DV)
        x = hm_ref[:, vs]
        d = x - jnp.mean(x, axis=-1, keepdims=True)
        var = jnp.mean(d * d, axis=-1, keepdims=True)
        y = d * lax.rsqrt(var + HEAD_NORM_EPS) * mnorm_ref[:, vs]
        parts.append((_sigmoid(mo_ref[:, vs]) * y).astype(BF16))
        yield
    mix = _dot(jnp.concatenate(parts, axis=1), wout_ref[...])
    for r in _row_groups(h1_ref.shape[0]):
        h2_ref[slot, r, :] = _layer_norm(ALPHA * h1_ref[r, :] + mix[r, :],
                                         lng_ref[...], lnb_ref[...])
        yield


def _mixer_core(proj_ref, wlr_ref, blr_ref, convw_ref, convb_ref, wqk_ref, gbias_ref,
                st_ref, c_ref, mw_ref, mg_ref, ext_ref, bc_ref, bcol_ref, acol_ref, amax_ref,
                mqk_ref, ogla_ref, hm_ref):
    rows_total = proj_ref.shape[0]
    n_chunks = rows_total // CHUNK

    ext_ref[8:, :] = proj_ref[:, MU0:MU0 + VAL_W]
    for lo in range(0, rows_total, PRE_ROWS):
        rg = slice(lo, lo + PRE_ROWS)
        small = proj_ref[rg, SM0:SM0 + SMALL_W]
        z = _dot(small.astype(BF16), wlr_ref[...]) + blr_ref[...]
        bc_ref[rg, :] = _chunk_scan(_log_sigmoid(z) * (1.0 / GLA_TAU), jnp.add, 0.0)
        yield
        gates = small + gbias_ref[...]
        b_gate = _chunk_scan(_log_sigmoid(gates), jnp.add, 0.0)
        a_gate = pltpu.roll(gates, F_LANE - I_LANE, axis=1) - b_gate
        bcol_ref[rg, :] = b_gate
        acol_ref[rg, :] = a_gate
        amax_ref[rg, :] = _chunk_scan(a_gate, jnp.maximum, -jnp.inf)
        conv = ext_ref[8 + lo:8 + lo + PRE_ROWS, :] * convw_ref[CONV_WIDTH - 1:CONV_WIDTH, :]
        for lag in range(1, CONV_WIDTH):
            conv = conv + (ext_ref[8 + lo - lag:8 + lo - lag + PRE_ROWS, :]
                           * convw_ref[CONV_WIDTH - 1 - lag:CONV_WIDTH - lag, :])
        cact = _silu(conv + convb_ref[...]).astype(BF16)
        mqk_ref[rg, :] = _dot(cact, wqk_ref[...])
        yield
    ext_ref[0:8, :] = ext_ref[rows_total:rows_total + 8, :]

    lane_w = lax.broadcasted_iota(jnp.int32, (CHUNK, KEY_W), 1) & (DK - 1)
    t_w = lax.broadcasted_iota(jnp.int32, (CHUNK, KEY_W), 0)
    tril_w = t_w >= lane_w
    diag_w = t_w == lane_w
    zero_v = jnp.zeros((CHUNK, DV), BF16)
    ones_v = jnp.ones((CHUNK, DV), BF16)

    def chunk_local(ci):
        rows = slice(ci * CHUNK, (ci + 1) * CHUNK)
        loc = {}
        q = proj_ref[rows, Q0:Q0 + KEY_W] * QK_SCALE
        k = proj_ref[rows, K0:K0 + KEY_W]
        b = bc_ref[rows, :]
        b_last = b[CHUNK - 1:CHUNK, :]
        b_mid = b[CHUNK // 2 - 1:CHUNK // 2, :]
        loc["qe"] = (q * jnp.exp(b)).astype(BF16)
        qr = (q * jnp.exp(b - b_mid)).astype(BF16)
        kr = (k * jnp.exp(b_mid - b)).astype(BF16)
        kd = (k * jnp.exp(b_last - b)).astype(BF16)
        loc["dec"] = jnp.exp(b_last)
        loc["gla"] = []
        for p in range(PAIRS):
            lp = slice(p * LANES, (p + 1) * LANES)
            va = proj_ref[rows, V0 + 2 * p * DV:V0 + (2 * p + 1) * DV].astype(BF16)
            vb = proj_ref[rows, V0 + (2 * p + 1) * DV:V0 + (2 * p + 2) * DV].astype(BF16)
            sc = _dot_nt(qr[:, lp], _pair_block_diag(kr[:, lp]))
            v_stack = jnp.concatenate([va, vb], axis=0)
            upd = _dot_tn(v_stack, _pair_block_diag(kd[:, lp]))
            loc["gla"].append((va, vb, sc, upd))
        a_w = _heads_to_lanes(acol_ref[rows, :])
        amax_w = _heads_to_lanes(amax_ref[rows, :])
        a_end = amax_w[CHUNK - 1:CHUNK, :]
        loc["amax_w"] = amax_w
        loc["a_end"] = a_end
        loc["a_row"] = jnp.sum(jnp.where(diag_w, a_w, 0.0), axis=0, keepdims=True)
        mq = mqk_ref[rows, 0:KEY_W] * QK_SCALE
        mk = mqk_ref[rows, KEY_W:2 * KEY_W]
        loc["mq"] = mq
        qb = mq.astype(BF16)
        kb = mk.astype(BF16)
        kw = (mk * jnp.exp(a_w - a_end)).astype(BF16)
        loc["mlstm"] = []
        for p in range(PAIRS):
            lp = slice(p * LANES, (p + 1) * LANES)
            va = proj_ref[rows, MV0 + 2 * p * DV:MV0 + (2 * p + 1) * DV].astype(BF16)
            vb = proj_ref[rows, MV0 + (2 * p + 1) * DV:MV0 + (2 * p + 2) * DV].astype(BF16)
            sqk = _dot_nt(qb[:, lp], _pair_block_diag(kb[:, lp]))
            vaug_stack = jnp.concatenate([jnp.concatenate([va, ones_v], axis=1),
                                          jnp.concatenate([vb, ones_v], axis=1)], axis=0)
            upd = _dot_tn(vaug_stack, _pair_block_diag(kw[:, lp]))
            loc["mlstm"].append((va, vb, sqk, upd))
        return loc

    def chunk_state(ci, loc):
        rows = slice(ci * CHUNK, (ci + 1) * CHUNK)
        for p in range(PAIRS):
            lp = slice(p * LANES, (p + 1) * LANES)
            va, vb, sc, upd = loc["gla"][p]
            sc = jnp.where(tril_w[:, lp], sc, 0.0).astype(BF16)
            v_bd = jnp.concatenate([jnp.concatenate([va, zero_v], axis=1),
                                    jnp.concatenate([zero_v, vb], axis=1)], axis=0)
            st = st_ref[p]
            out = _dot_nt(loc["qe"][:, lp], _pair_block_diag(st.astype(BF16))) + _dot(sc, v_bd)
            ogla_ref[rows, 2 * p * DV:(2 * p + 2) * DV] = out
            st_ref[p] = st * loc["dec"][:, lp] + upd

        m_prev_g = mg_ref[0:1, :]
        m_prev_w = mw_ref[0:1, :]
        mt_col = bcol_ref[rows, :] + jnp.maximum(m_prev_g, amax_ref[rows, :])
        big_m = jnp.maximum(m_prev_w, loc["amax_w"])
        e_mat = jnp.exp(jnp.where(tril_w, loc["a_row"] - big_m, -jnp.inf))
        m_end = big_m[CHUNK - 1:CHUNK, :]
        w_state = jnp.exp(m_prev_w - m_end)
        u_scale = jnp.exp(loc["a_end"] - m_end)
        qw = (loc["mq"] * jnp.exp(m_prev_w - big_m)).astype(BF16)
        for p in range(PAIRS):
            lp = slice(p * LANES, (p + 1) * LANES)
            va, vb, sqk, upd = loc["mlstm"][p]
            pm = (sqk * e_mat[:, lp]).astype(BF16)
            vaug_bd = jnp.concatenate(
                [jnp.concatenate([va, ones_v, zero_v, zero_v], axis=1),
                 jnp.concatenate([zero_v, zero_v, vb, ones_v], axis=1)], axis=0)
            ct = c_ref[p]
            nd = _dot_nt(qw[:, lp], _pair_block_diag(ct.astype(BF16))) + _dot(pm, vaug_bd)
            for j in range(2):
                h = 2 * p + j
                num = nd[:, 2 * j * DV:(2 * j + 1) * DV]
                den = nd[:, (2 * j + 1) * DV:(2 * j + 2) * DV]
                m_t = jnp.broadcast_to(mt_col[:, F_LANE + h:F_LANE + h + 1], (CHUNK, DV))
                hm_ref[rows, h * DV:(h + 1) * DV] = num / jnp.maximum(jnp.abs(den), jnp.exp(-m_t))
            c_ref[p] = ct * w_state[:, lp] + upd * u_scale[:, lp]
        m_new = mt_col[CHUNK - 8:CHUNK, :]
        mg_ref[...] = jnp.broadcast_to(m_new[7:8, :], (8, LANES))
        mw_ref[...] = jnp.broadcast_to(_heads_to_lanes(m_new)[7:8, :], (8, KEY_W))

    pending = None
    for ci in range(n_chunks + 1):
        nxt = chunk_local(ci) if ci < n_chunks else None
        if pending is not None:
            chunk_state(ci - 1, pending)
        pending = nxt
        yield


N_CORE_PARAMS = 6
N_EPI_PARAMS = 5
N_FFN_PARAMS = 5


def _mix_ffn_body(*refs, blocks_per_seq):
    proj_ref, gg_ref, mo_ref, h1_ref = refs[0:4]
    pos = 4
    core_params = refs[pos:pos + N_CORE_PARAMS]
    pos += N_CORE_PARAMS
    epi_params = refs[pos:pos + N_EPI_PARAMS]
    pos += N_EPI_PARAMS
    wg_ref, wu_ref, wd_ref, ln3g_ref, ln3b_ref = refs[pos:pos + N_FFN_PARAMS]
    pos += N_FFN_PARAMS
    out_ref = refs[pos]
    scratch = refs[pos + 1:]
    hbuf_ref = scratch[0]
    z_ref = scratch[1]
    core_scratch = scratch[2:]
    st_ref, c_ref, mw_ref, mg_ref, ext_ref = core_scratch[0:5]
    ogla_ref, hm_ref = core_scratch[-2:]
    g = pl.program_id(0)
    slot = lax.rem(g, 2)

    @pl.when(g == 0)
    def _():
        hbuf_ref[1] = jnp.zeros(hbuf_ref.shape[1:], F32)
        z_ref[...] = jnp.zeros_like(z_ref)
        ogla_ref[...] = jnp.zeros_like(ogla_ref)
        hm_ref[...] = jnp.zeros_like(hm_ref)

    @pl.when(lax.rem(g, blocks_per_seq) == 0)
    def _():
        st_ref[...] = jnp.zeros_like(st_ref)
        c_ref[...] = jnp.zeros_like(c_ref)
        mw_ref[...] = jnp.zeros_like(mw_ref)
        mg_ref[...] = jnp.zeros_like(mg_ref)
        ext_ref[0:8, :] = jnp.zeros((8, VAL_W), F32)

    def mixer_stream():
        yield from _mixer_epilogue(gg_ref, mo_ref, h1_ref, *epi_params, ogla_ref, hm_ref,
                                   hbuf_ref, slot)
        yield from _mixer_core(proj_ref, *core_params, *core_scratch)

    def store_z(z):
        z_ref[...] = z
        yield

    def ffn_stream():
        for r in _row_groups(z_ref.shape[0]):
            out_ref[r, :] = _layer_norm(z_ref[r, :], ln3g_ref[...], ln3b_ref[...])
            yield
        yield from _ffn_stages(lambda: hbuf_ref[1 - slot], wg_ref, wu_ref, wd_ref, store_z)

    n_ln_stages = len(_row_groups(MIX_ROWS))
    n_mix_stages = (2 * HEADS + n_ln_stages + 2 * (MIX_ROWS // PRE_ROWS)
                    + (proj_ref.shape[0] // CHUNK + 1))
    _run_interleaved((mixer_stream(), n_mix_stages, 0.0, 1.0),
                     (ffn_stream(), n_ln_stages + _n_ffn_stages() + 1, 0.0, 1.0))


def _mix_ffn_call(proj, h1, core_params, epi_params, ffn_params, seq):
    t = proj.shape[0]
    n_blocks = t // MIX_ROWS
    last = n_blocks - 1
    cur = lambda g: (jnp.minimum(g, last), 0)
    prev_rows = lambda g: jnp.clip(g - 1, 0, last)
    assert (len(core_params), len(epi_params), len(ffn_params)) == (
        N_CORE_PARAMS, N_EPI_PARAMS, N_FFN_PARAMS)
    in_specs = [
        pl.BlockSpec((MIX_ROWS, PROJ_W), cur),
        pl.BlockSpec((MIX_ROWS, VAL_W), lambda g: (prev_rows(g), G0 // VAL_W)),
        pl.BlockSpec((MIX_ROWS, VAL_W), lambda g: (prev_rows(g), MO0 // VAL_W)),
        pl.BlockSpec((MIX_ROWS, D_MODEL), lambda g: (prev_rows(g), 0)),
    ]
    in_specs += [_resident(p.shape) for p in (*core_params, *epi_params, *ffn_params)]
    scratch = [
        pltpu.VMEM((2, MIX_ROWS, D_MODEL), F32),
        pltpu.VMEM((MIX_ROWS, D_MODEL), F32),
        pltpu.VMEM((PAIRS, DV, LANES), F32),
        pltpu.VMEM((PAIRS, 2 * DV, LANES), F32),
        pltpu.VMEM((8, KEY_W), F32),
        pltpu.VMEM((8, LANES), F32),
        pltpu.VMEM((MIX_ROWS + 8, VAL_W), F32),
        pltpu.VMEM((MIX_ROWS, KEY_W), F32),
        pltpu.VMEM((MIX_ROWS, SMALL_W), F32),
        pltpu.VMEM((MIX_ROWS, SMALL_W), F32),
        pltpu.VMEM((MIX_ROWS, SMALL_W), F32),
        pltpu.VMEM((MIX_ROWS, 2 * KEY_W), F32),
        pltpu.VMEM((MIX_ROWS, VAL_W), F32),
        pltpu.VMEM((MIX_ROWS, VAL_W), F32),
    ]
    return pl.pallas_call(
        functools.partial(_mix_ffn_body, blocks_per_seq=seq // MIX_ROWS),
        grid=(n_blocks + PIPE_DEPTH,),
        in_specs=in_specs,
        out_specs=pl.BlockSpec((MIX_ROWS, D_MODEL), lambda g: (jnp.maximum(g - PIPE_DEPTH, 0), 0)),
        out_shape=jax.ShapeDtypeStruct((t, D_MODEL), F32),
        scratch_shapes=scratch,
        compiler_params=pltpu.CompilerParams(
            dimension_semantics=("arbitrary",), vmem_limit_bytes=VMEM_LIMIT),
        name="mix_ffn",
    )(proj, proj, proj, h1, *core_params, *epi_params, *ffn_params)


def _prep_body(*refs):
    n_plain = (len(refs) - 3) // 2
    srcs, (win_src, small_src) = refs[:n_plain], refs[n_plain:n_plain + 2]
    dsts, win_dst = refs[n_plain + 2:-1], refs[-1]
    for src, dst in zip(srcs, dsts):
        dst[...] = src[...].astype(dst.dtype)
    head_w = G0 + VAL_W
    body_w = SM0 - MU0
    win_dst[:, 0:head_w] = win_src[:, 0:head_w].astype(BF16)
    win_dst[:, MU0:SM0] = win_src[:, head_w + GLA_RANK:head_w + GLA_RANK + body_w].astype(BF16)
    win_dst[:, SM0:PROJ_W] = small_src[...].astype(BF16)


def _prep_weights(plain, w_in, small):
    def spec(rows, cols):
        assert rows % (16 * CAST_STEPS) == 0
        return pl.BlockSpec((rows // CAST_STEPS, cols), lambda i: (i, 0))
    rows = w_in.shape[0]
    return pl.pallas_call(
        _prep_body,
        grid=(CAST_STEPS,),
        in_specs=[spec(*a.shape) for a in (*plain, w_in, small)],
        out_specs=[spec(*a.shape) for a in plain] + [spec(rows, PROJ_W)],
        out_shape=[jax.ShapeDtypeStruct(a.shape, BF16) for a in plain]
        + [jax.ShapeDtypeStruct((rows, PROJ_W), BF16)],
        compiler_params=pltpu.CompilerParams(
            dimension_semantics=("arbitrary",), vmem_limit_bytes=VMEM_LIMIT),
        name="prep_weights",
    )(*plain, w_in, small)


def _block_diag(w):
    h, dv, dk = w.shape
    eye = jnp.eye(h, dtype=w.dtype)
    return (w[:, :, None, :] * eye[:, None, :, None]).reshape(h * dv, h * dk)


def kernel(x, ln1_g, ln1_b, ffn1_w_gate, ffn1_w_up, ffn1_w_down, w_in, gla_w_lr, gla_b_lr,
           gla_norm_g, mlstm_conv_w, mlstm_conv_b, mlstm_w_q, mlstm_w_k, mlstm_b_i, mlstm_b_f,
           mlstm_norm_g, w_out, ln2_g, ln2_b, ffn2_w_gate, ffn2_w_up, ffn2_w_down, ln3_g, ln3_b):
    batch, seq, d = x.shape
    assert d == D_MODEL and seq % MIX_ROWS == 0 and (batch * seq) % FFN_PROJ_ROWS == 0
    assert ln1_g.shape[0] == 1, "single-layer problem"
    row = lambda v: v.reshape(1, -1).astype(F32)

    wi = w_in[0]
    lr0 = G0 + VAL_W
    gate0 = wi.shape[1] - 2 * HEADS
    small = jnp.concatenate(
        [wi[:, lr0:lr0 + GLA_RANK], wi[:, gate0:],
         jnp.zeros((D_MODEL, SMALL_W - GLA_RANK - 2 * HEADS), wi.dtype)], axis=1)
    w1g, w1u, w1d, w2g, w2u, w2d, wo, win = _prep_weights(
        [ffn1_w_gate[0], ffn1_w_up[0], ffn1_w_down[0], ffn2_w_gate[0], ffn2_w_up[0],
         ffn2_w_down[0], w_out[0]], wi, small)

    wlr = jnp.zeros((SMALL_W, KEY_W), F32).at[LR_LANE:LR_LANE + GLA_RANK].set(gla_w_lr[0])
    gbias = (jnp.zeros((1, SMALL_W), F32)
             .at[0, I_LANE:I_LANE + HEADS].set(mlstm_b_i[0])
             .at[0, F_LANE:F_LANE + HEADS].set(mlstm_b_f[0]))
    wqk = jnp.concatenate([_block_diag(mlstm_w_q[0]), _block_diag(mlstm_w_k[0])], axis=1)

    xf = x.reshape(batch * seq, d)
    h1, proj = _ffn_proj_call(xf, w1g, w1u, w1d, row(ln1_g), row(ln1_b), win)
    core_params = [wlr.astype(BF16), row(gla_b_lr), mlstm_conv_w[0].astype(F32),
                   row(mlstm_conv_b), wqk.astype(BF16), gbias]
    epi_params = [row(gla_norm_g), row(mlstm_norm_g), wo, row(ln2_g), row(ln2_b)]
    ffn_params = [w2g, w2u, w2d, row(ln3_g), row(ln3_b)]
    h3 = _mix_ffn_call(proj, h1, core_params, epi_params, ffn_params, seq)
    return h3.reshape(batch, seq, d)
```

```python
import functools

import jax
import jax.numpy as jnp
from jax import lax
from jax.experimental import pallas as pl
from jax.experimental.pallas import tpu as pltpu

F32 = jnp.float32
BF16 = jnp.bfloat16

D_MODEL = 1024
D_FF = 2816
HEADS = 4
PAIRS = HEADS // 2
DK = 64
DV = 128
KEY_W = HEADS * DK
VAL_W = HEADS * DV
LANES = 128
SUBLANES = 8
BF16_ROWS = 16
GLA_RANK = 16
GLA_TAU = 16.0
CONV_WIDTH = 4
CHUNK = 64
ALPHA = 2.0 ** 0.25
LN_EPS = 1e-5
HEAD_NORM_EPS = 1e-6
QK_SCALE = DK ** -0.5

Q0 = 0
K0 = Q0 + KEY_W
V0 = K0 + KEY_W
G0 = V0 + VAL_W
MU0 = G0 + VAL_W
MV0 = MU0 + VAL_W
MO0 = MV0 + VAL_W
SM0 = MO0 + VAL_W
SMALL_W = LANES
LR_LANE = 0
I_LANE = GLA_RANK
F_LANE = I_LANE + HEADS
PROJ_W = SM0 + SMALL_W

FFN_PROJ_ROWS = 512
FFN_SUB = 256
FFN_COLS = 512
PROJ_COLS = 1024
SUB_TILE_LAG = 0.15
MIX_ROWS = 256
CAST_STEPS = 8
VMEM_LIMIT = 60 * 1024 * 1024


def _layer_norm(z, g, b):
    mu = jnp.mean(z, axis=-1, keepdims=True)
    d = z - mu
    var = jnp.mean(d * d, axis=-1, keepdims=True)
    return d * lax.rsqrt(var + LN_EPS) * g + b


def _sigmoid(x):
    return 0.5 * jnp.tanh(0.5 * x) + 0.5


def _silu(x):
    h = 0.5 * x
    return h * jnp.tanh(h) + h


def _log_sigmoid(x):
    return jnp.minimum(x, 0.0) - jnp.log(1.0 + jnp.exp(-jnp.abs(x)))


def _dot(a, b):
    return jnp.dot(a, b, preferred_element_type=F32)


def _dot_nt(a, b):
    return lax.dot_general(a, b, (((1,), (1,)), ((), ())), preferred_element_type=F32)


def _dot_tn(a, b):
    return lax.dot_general(a, b, (((0,), (0,)), ((), ())), preferred_element_type=F32)


def _ffn_col_chunks():
    bounds = [min(c, D_FF) for c in range(0, D_FF + FFN_COLS, FFN_COLS)]
    return [slice(lo, hi) for lo, hi in zip(bounds[:-1], bounds[1:]) if hi > lo]


def _ffn_stages(load_x, wg_ref, wu_ref, wd_ref, lng_ref, lnb_ref, finish):
    xb = load_x().astype(BF16)
    col_chunks = _ffn_col_chunks()
    y = None
    gu = None
    for c in range(len(col_chunks) + 1):
        gu_next = None
        if c < len(col_chunks):
            cs = col_chunks[c]
            gu_next = (_dot(xb, wg_ref[:, cs]), _dot(xb, wu_ref[:, cs]))
        if gu is not None:
            part = _dot((_silu(gu[0]) * gu[1]).astype(BF16), wd_ref[col_chunks[c - 1], :])
            y = part if y is None else y + part
        gu = gu_next
        yield
    yield from finish(_layer_norm(ALPHA * load_x() + 0.5 * y, lng_ref[...], lnb_ref[...]))


def _n_ffn_stages():
    return len(_ffn_col_chunks()) + 1


def _run_interleaved(*streams):
    order = []
    for i, (_, n, start, end) in enumerate(streams):
        order += [(start + (k + 0.5) / n * (end - start), i) for k in range(n)]
    for _, i in sorted(order):
        next(streams[i][0], None)
    for gen, _, _, _ in streams:
        for _ in gen:
            pass


def _resident(shape):
    return pl.BlockSpec(shape, lambda *_: (0,) * len(shape), pipeline_mode=pl.Buffered(1))


def _ffn_proj_body(x_ref, wg_ref, wu_ref, wd_ref, lng_ref, lnb_ref, win_ref, h_ref, proj_ref):
    n_sub = x_ref.shape[0] // FFN_SUB
    proj_chunks = [slice(lo, min(lo + PROJ_COLS, PROJ_W)) for lo in range(0, PROJ_W, PROJ_COLS)]

    def make_stream(s):
        r = slice(s * FFN_SUB, (s + 1) * FFN_SUB)

        def finish(h):
            h_ref[r, :] = h
            hb = h.astype(BF16)
            for cs in proj_chunks:
                proj_ref[r, cs] = _dot(hb, win_ref[:, cs])
                yield

        return _ffn_stages(lambda: x_ref[r, :], wg_ref, wu_ref, wd_ref, lng_ref, lnb_ref, finish)

    n_stages = _n_ffn_stages() + len(proj_chunks)
    lag = SUB_TILE_LAG / max(n_sub - 1, 1)
    _run_interleaved(*[(make_stream(s), n_stages, s * lag, 1.0 - (n_sub - 1 - s) * lag)
                       for s in range(n_sub)])


def _ffn_proj_call(x, wg, wu, wd, lng, lnb, win):
    t = x.shape[0]
    row = lambda i: (i, 0)
    params = [wg, wu, wd, lng, lnb, win]
    return pl.pallas_call(
        _ffn_proj_body,
        grid=(t // FFN_PROJ_ROWS,),
        in_specs=[pl.BlockSpec((FFN_PROJ_ROWS, D_MODEL), row)] + [_resident(p.shape) for p in params],
        out_specs=[pl.BlockSpec((FFN_PROJ_ROWS, D_MODEL), row),
                   pl.BlockSpec((FFN_PROJ_ROWS, PROJ_W), row)],
        out_shape=[jax.ShapeDtypeStruct((t, D_MODEL), F32),
                   jax.ShapeDtypeStruct((t, PROJ_W), F32)],
        compiler_params=pltpu.CompilerParams(
            dimension_semantics=("arbitrary",), vmem_limit_bytes=VMEM_LIMIT),
        name="ffn_proj",
    )(x, *params)


def _chunk_scan(x, combine, fill):
    row_in_chunk = lax.broadcasted_iota(jnp.int32, x.shape, 0) & (CHUNK - 1)
    shift = 1
    while shift < CHUNK:
        x = combine(x, jnp.where(row_in_chunk >= shift, pltpu.roll(x, shift, axis=0), fill))
        shift *= 2
    return x


def _pair_block_diag(x):
    left = lax.broadcasted_iota(jnp.int32, x.shape, 1) < DK
    zero = jnp.zeros_like(x)
    return jnp.concatenate([jnp.where(left, x, zero), jnp.where(left, zero, x)], axis=0)


def _heads_to_lanes(x):
    left = lax.broadcasted_iota(jnp.int32, (x.shape[0], LANES), 1) < DK
    cols = [jnp.broadcast_to(x[:, F_LANE + h:F_LANE + h + 1], (x.shape[0], LANES))
            for h in range(HEADS)]
    return jnp.concatenate([jnp.where(left, cols[0], cols[1]),
                            jnp.where(left, cols[2], cols[3])], axis=1)


def _mixer_block(proj_ref, h1_ref, wlr_ref, blr_ref, gnorm_ref, convw_ref, convb_ref, wqk_ref,
                 gbias_ref, mnorm_ref, wout_ref, lng_ref, lnb_ref,
                 st_ref, c_ref, mw_ref, mg_ref, ext_ref, bc_ref, bcol_ref, acol_ref, amax_ref,
                 mqk_ref, ogla_ref, hm_ref, h2_ref, slot):
    rows_total = proj_ref.shape[0]
    n_chunks = rows_total // CHUNK

    small = proj_ref[:, SM0:SM0 + SMALL_W]
    z = _dot(small.astype(BF16), wlr_ref[...]) + blr_ref[...]
    bc_ref[...] = _chunk_scan(_log_sigmoid(z) * (1.0 / GLA_TAU), jnp.add, 0.0)
    yield

    gates = small + gbias_ref[...]
    b_gate = _chunk_scan(_log_sigmoid(gates), jnp.add, 0.0)
    a_gate = pltpu.roll(gates, F_LANE - I_LANE, axis=1) - b_gate
    bcol_ref[...] = b_gate
    acol_ref[...] = a_gate
    amax_ref[...] = _chunk_scan(a_gate, jnp.maximum, -jnp.inf)
    yield

    ext_ref[SUBLANES:, :] = proj_ref[:, MU0:MU0 + VAL_W]
    conv = ext_ref[SUBLANES:, :] * convw_ref[CONV_WIDTH - 1:CONV_WIDTH, :]
    for lag in range(1, CONV_WIDTH):
        conv = conv + (ext_ref[SUBLANES - lag:SUBLANES - lag + rows_total, :]
                       * convw_ref[CONV_WIDTH - 1 - lag:CONV_WIDTH - lag, :])
    ext_ref[0:SUBLANES, :] = ext_ref[rows_total:rows_total + SUBLANES, :]
    cact = _silu(conv + convb_ref[...]).astype(BF16)
    mqk_ref[...] = _dot(cact, wqk_ref[...])
    yield

    lane_w = lax.broadcasted_iota(jnp.int32, (CHUNK, KEY_W), 1) & (DK - 1)
    t_w = lax.broadcasted_iota(jnp.int32, (CHUNK, KEY_W), 0)
    tril_w = t_w >= lane_w
    diag_w = t_w == lane_w
    zero_v = jnp.zeros((CHUNK, DV), BF16)
    ones_v = jnp.ones((CHUNK, DV), BF16)

    def chunk_local(ci):
        rows = slice(ci * CHUNK, (ci + 1) * CHUNK)
        loc = {}
        q = proj_ref[rows, Q0:Q0 + KEY_W] * QK_SCALE
        k = proj_ref[rows, K0:K0 + KEY_W]
        b = bc_ref[rows, :]
        b_last = b[CHUNK - 1:CHUNK, :]
        b_mid = b[CHUNK // 2 - 1:CHUNK // 2, :]
        loc["qe"] = (q * jnp.exp(b)).astype(BF16)
        qr = (q * jnp.exp(b - b_mid)).astype(BF16)
        kr = (k * jnp.exp(b_mid - b)).astype(BF16)
        kd = (k * jnp.exp(b_last - b)).astype(BF16)
        loc["dec"] = jnp.exp(b_last)
        loc["gla"] = []
        for p in range(PAIRS):
            lp = slice(p * LANES, (p + 1) * LANES)
            va = proj_ref[rows, V0 + 2 * p * DV:V0 + (2 * p + 1) * DV].astype(BF16)
            vb = proj_ref[rows, V0 + (2 * p + 1) * DV:V0 + (2 * p + 2) * DV].astype(BF16)
            sc = _dot_nt(qr[:, lp], _pair_block_diag(kr[:, lp]))
            v_stack = jnp.concatenate([va, vb], axis=0)
            upd = _dot_tn(v_stack, _pair_block_diag(kd[:, lp]))
            loc["gla"].append((va, vb, sc, upd))
        a_w = _heads_to_lanes(acol_ref[rows, :])
        amax_w = _heads_to_lanes(amax_ref[rows, :])
        a_end = amax_w[CHUNK - 1:CHUNK, :]
        loc["amax_w"] = amax_w
        loc["a_end"] = a_end
        loc["a_row"] = jnp.sum(jnp.where(diag_w, a_w, 0.0), axis=0, keepdims=True)
        mq = mqk_ref[rows, 0:KEY_W] * QK_SCALE
        mk = mqk_ref[rows, KEY_W:2 * KEY_W]
        loc["mq"] = mq
        qb = mq.astype(BF16)
        kb = mk.astype(BF16)
        kw = (mk * jnp.exp(a_w - a_end)).astype(BF16)
        loc["mlstm"] = []
        for p in range(PAIRS):
            lp = slice(p * LANES, (p + 1) * LANES)
            va = proj_ref[rows, MV0 + 2 * p * DV:MV0 + (2 * p + 1) * DV].astype(BF16)
            vb = proj_ref[rows, MV0 + (2 * p + 1) * DV:MV0 + (2 * p + 2) * DV].astype(BF16)
            sqk = _dot_nt(qb[:, lp], _pair_block_diag(kb[:, lp]))
            vaug_stack = jnp.concatenate([jnp.concatenate([va, ones_v], axis=1),
                                          jnp.concatenate([vb, ones_v], axis=1)], axis=0)
            upd = _dot_tn(vaug_stack, _pair_block_diag(kw[:, lp]))
            loc["mlstm"].append((va, vb, sqk, upd))
        return loc

    def chunk_state(ci, loc):
        rows = slice(ci * CHUNK, (ci + 1) * CHUNK)
        for p in range(PAIRS):
            lp = slice(p * LANES, (p + 1) * LANES)
            va, vb, sc, upd = loc["gla"][p]
            sc = jnp.where(tril_w[:, lp], sc, 0.0).astype(BF16)
            v_bd = jnp.concatenate([jnp.concatenate([va, zero_v], axis=1),
                                    jnp.concatenate([zero_v, vb], axis=1)], axis=0)
            st = st_ref[p]
            out = _dot_nt(loc["qe"][:, lp], _pair_block_diag(st.astype(BF16))) + _dot(sc, v_bd)
            ogla_ref[rows, 2 * p * DV:(2 * p + 2) * DV] = out
            st_ref[p] = st * loc["dec"][:, lp] + upd

        m_prev_g = mg_ref[0:1, :]
        m_prev_w = mw_ref[0:1, :]
        mt_col = bcol_ref[rows, :] + jnp.maximum(m_prev_g, amax_ref[rows, :])
        big_m = jnp.maximum(m_prev_w, loc["amax_w"])
        e_mat = jnp.exp(jnp.where(tril_w, loc["a_row"] - big_m, -jnp.inf))
        m_end = big_m[CHUNK - 1:CHUNK, :]
        w_state = jnp.exp(m_prev_w - m_end)
        u_scale = jnp.exp(loc["a_end"] - m_end)
        qw = (loc["mq"] * jnp.exp(m_prev_w - big_m)).astype(BF16)
        for p in range(PAIRS):
            lp = slice(p * LANES, (p + 1) * LANES)
            va, vb, sqk, upd = loc["mlstm"][p]
            pm = (sqk * e_mat[:, lp]).astype(BF16)
            vaug_bd = jnp.concatenate(
                [jnp.concatenate([va, ones_v, zero_v, zero_v], axis=1),
                 jnp.concatenate([zero_v, zero_v, vb, ones_v], axis=1)], axis=0)
            ct = c_ref[p]
            nd = _dot_nt(qw[:, lp], _pair_block_diag(ct.astype(BF16))) + _dot(pm, vaug_bd)
            for j in range(2):
                h = 2 * p + j
                num = nd[:, 2 * j * DV:(2 * j + 1) * DV]
                den = nd[:, (2 * j + 1) * DV:(2 * j + 2) * DV]
                m_t = jnp.broadcast_to(mt_col[:, F_LANE + h:F_LANE + h + 1], (CHUNK, DV))
                hm_ref[rows, h * DV:(h + 1) * DV] = num / jnp.maximum(jnp.abs(den), jnp.exp(-m_t))
            c_ref[p] = ct * w_state[:, lp] + upd * u_scale[:, lp]
        m_new = mt_col[CHUNK - SUBLANES:CHUNK, :]
        mg_ref[...] = jnp.broadcast_to(m_new[SUBLANES - 1:, :], (SUBLANES, LANES))
        mw_ref[...] = jnp.broadcast_to(_heads_to_lanes(m_new)[SUBLANES - 1:, :],
                                       (SUBLANES, KEY_W))

    pending = None
    for ci in range(n_chunks + 1):
        nxt = chunk_local(ci) if ci < n_chunks else None
        if pending is not None:
            chunk_state(ci - 1, pending)
        pending = nxt
        yield

    parts = []
    for h in range(HEADS):
        vs = slice(h * DV, (h + 1) * DV)
        o = ogla_ref[:, vs]
        y = o * lax.rsqrt(jnp.mean(o * o, axis=-1, keepdims=True) + HEAD_NORM_EPS)
        y = y * gnorm_ref[:, vs]
        parts.append(y * _silu(proj_ref[:, G0 + h * DV:G0 + (h + 1) * DV]))
    yield
    for h in range(HEADS):
        vs = slice(h * DV, (h + 1) * DV)
        x = hm_ref[:, vs]
        d = x - jnp.mean(x, axis=-1, keepdims=True)
        var = jnp.mean(d * d, axis=-1, keepdims=True)
        y = d * lax.rsqrt(var + HEAD_NORM_EPS) * mnorm_ref[:, vs]
        parts.append(_sigmoid(proj_ref[:, MO0 + h * DV:MO0 + (h + 1) * DV]) * y)
    yield
    y = jnp.concatenate(parts, axis=1).astype(BF16)
    mix = _dot(y, wout_ref[...])
    h2_ref[slot] = _layer_norm(ALPHA * h1_ref[...] + mix, lng_ref[...], lnb_ref[...])


N_MIX_PARAMS = 11
N_FFN_PARAMS = 5


def _mix_ffn_body(*refs, blocks_per_seq):
    proj_ref, h1_ref = refs[0:2]
    mix_params = refs[2:2 + N_MIX_PARAMS]
    ffn_params = refs[2 + N_MIX_PARAMS:2 + N_MIX_PARAMS + N_FFN_PARAMS]
    out_ref = refs[2 + N_MIX_PARAMS + N_FFN_PARAMS]
    scratch = refs[3 + N_MIX_PARAMS + N_FFN_PARAMS:]
    hbuf_ref = scratch[0]
    mix_scratch = scratch[1:]
    st_ref, c_ref, mw_ref, mg_ref, ext_ref = mix_scratch[0:5]
    g = pl.program_id(0)
    slot = lax.rem(g, 2)

    @pl.when(g == 0)
    def _():
        hbuf_ref[1] = jnp.zeros(hbuf_ref.shape[1:], F32)

    @pl.when(lax.rem(g, blocks_per_seq) == 0)
    def _():
        st_ref[...] = jnp.zeros_like(st_ref)
        c_ref[...] = jnp.zeros_like(c_ref)
        mw_ref[...] = jnp.zeros_like(mw_ref)
        mg_ref[...] = jnp.zeros_like(mg_ref)
        ext_ref[0:SUBLANES, :] = jnp.zeros((SUBLANES, VAL_W), F32)

    def store_out(h):
        out_ref[...] = h
        yield

    n_mix_stages = proj_ref.shape[0] // CHUNK + 7
    _run_interleaved(
        (_mixer_block(proj_ref, h1_ref, *mix_params, *mix_scratch, hbuf_ref, slot),
         n_mix_stages, 0.0, 1.0),
        (_ffn_stages(lambda: hbuf_ref[1 - slot], *ffn_params, store_out),
         _n_ffn_stages() + 1, 0.0, 1.0))


def _mix_ffn_call(proj, h1, mix_params, ffn_params, seq):
    t = proj.shape[0]
    n_blocks = t // MIX_ROWS
    cur = lambda g: (jnp.minimum(g, n_blocks - 1), 0)
    prev = lambda g: (jnp.maximum(g - 1, 0), 0)
    assert len(mix_params) == N_MIX_PARAMS and len(ffn_params) == N_FFN_PARAMS
    in_specs = [pl.BlockSpec((MIX_ROWS, PROJ_W), cur), pl.BlockSpec((MIX_ROWS, D_MODEL), cur)]
    in_specs += [_resident(p.shape) for p in (*mix_params, *ffn_params)]
    scratch = [
        pltpu.VMEM((2, MIX_ROWS, D_MODEL), F32),
        pltpu.VMEM((PAIRS, DV, LANES), F32),
        pltpu.VMEM((PAIRS, 2 * DV, LANES), F32),
        pltpu.VMEM((SUBLANES, KEY_W), F32),
        pltpu.VMEM((SUBLANES, LANES), F32),
        pltpu.VMEM((MIX_ROWS + SUBLANES, VAL_W), F32),
        pltpu.VMEM((MIX_ROWS, KEY_W), F32),
        pltpu.VMEM((MIX_ROWS, SMALL_W), F32),
        pltpu.VMEM((MIX_ROWS, SMALL_W), F32),
        pltpu.VMEM((MIX_ROWS, SMALL_W), F32),
        pltpu.VMEM((MIX_ROWS, 2 * KEY_W), F32),
        pltpu.VMEM((MIX_ROWS, VAL_W), F32),
        pltpu.VMEM((MIX_ROWS, VAL_W), F32),
    ]
    return pl.pallas_call(
        functools.partial(_mix_ffn_body, blocks_per_seq=seq // MIX_ROWS),
        grid=(n_blocks + 1,),
        in_specs=in_specs,
        out_specs=pl.BlockSpec((MIX_ROWS, D_MODEL), prev),
        out_shape=jax.ShapeDtypeStruct((t, D_MODEL), F32),
        scratch_shapes=scratch,
        compiler_params=pltpu.CompilerParams(
            dimension_semantics=("arbitrary",), vmem_limit_bytes=VMEM_LIMIT),
        name="mix_ffn",
    )(proj, h1, *mix_params, *ffn_params)


def _prep_body(*refs):
    n_plain = (len(refs) - 2) // 2
    srcs, win_src = refs[:n_plain], refs[n_plain]
    dsts, win_dst = refs[n_plain + 1:-1], refs[-1]
    for src, dst in zip(srcs, dsts):
        dst[...] = src[...].astype(dst.dtype)
    head_w = G0 + VAL_W
    body_w = SM0 - MU0
    gate0 = head_w + GLA_RANK + body_w
    rows = win_dst.shape[0]
    win_dst[:, 0:head_w] = win_src[0, :, 0:head_w].astype(BF16)
    win_dst[:, MU0:SM0] = win_src[0, :, head_w + GLA_RANK:gate0].astype(BF16)
    win_dst[:, SM0:PROJ_W] = jnp.concatenate(
        [win_src[0, :, head_w:head_w + GLA_RANK], win_src[0, :, gate0:gate0 + 2 * HEADS],
         jnp.zeros((rows, SMALL_W - GLA_RANK - 2 * HEADS), F32)], axis=1).astype(BF16)


def _prep_weights(plain, w_in):
    def spec(rows, cols):
        assert rows % (BF16_ROWS * CAST_STEPS) == 0
        return pl.BlockSpec((rows // CAST_STEPS, cols), lambda i: (i, 0))
    _, rows, in_cols = w_in.shape
    win_spec = pl.BlockSpec((1, rows // CAST_STEPS, in_cols), lambda i: (0, i, 0))
    return pl.pallas_call(
        _prep_body,
        grid=(CAST_STEPS,),
        in_specs=[spec(*a.shape) for a in plain] + [win_spec],
        out_specs=[spec(*a.shape) for a in plain] + [spec(rows, PROJ_W)],
        out_shape=[jax.ShapeDtypeStruct(a.shape, BF16) for a in plain]
        + [jax.ShapeDtypeStruct((rows, PROJ_W), BF16)],
        compiler_params=pltpu.CompilerParams(
            dimension_semantics=("arbitrary",), vmem_limit_bytes=VMEM_LIMIT),
        name="prep_weights",
    )(*plain, w_in)


def _block_diag(w):
    h, dv, dk = w.shape
    eye = jnp.eye(h, dtype=w.dtype)
    return (w[:, :, None, :] * eye[:, None, :, None]).reshape(h * dv, h * dk)


def kernel(x, ln1_g, ln1_b, ffn1_w_gate, ffn1_w_up, ffn1_w_down, w_in, gla_w_lr, gla_b_lr,
           gla_norm_g, mlstm_conv_w, mlstm_conv_b, mlstm_w_q, mlstm_w_k, mlstm_b_i, mlstm_b_f,
           mlstm_norm_g, w_out, ln2_g, ln2_b, ffn2_w_gate, ffn2_w_up, ffn2_w_down, ln3_g, ln3_b):
    batch, seq, d = x.shape
    assert d == D_MODEL and seq % MIX_ROWS == 0 and (batch * seq) % FFN_PROJ_ROWS == 0
    assert ln1_g.shape[0] == 1, "single-layer problem"
    row = lambda v: v.reshape(1, -1).astype(F32)

    w1g, w1u, w1d, w2g, w2u, w2d, wo, win = _prep_weights(
        [ffn1_w_gate[0], ffn1_w_up[0], ffn1_w_down[0], ffn2_w_gate[0], ffn2_w_up[0],
         ffn2_w_down[0], w_out[0]], w_in)

    wlr = jnp.zeros((SMALL_W, KEY_W), F32).at[LR_LANE:LR_LANE + GLA_RANK].set(gla_w_lr[0])
    gbias = (jnp.zeros((1, SMALL_W), F32)
             .at[0, I_LANE:I_LANE + HEADS].set(mlstm_b_i[0])
             .at[0, F_LANE:F_LANE + HEADS].set(mlstm_b_f[0]))
    wqk = jnp.concatenate([_block_diag(mlstm_w_q[0]), _block_diag(mlstm_w_k[0])], axis=1)

    xf = x.reshape(batch * seq, d)
    h1, proj = _ffn_proj_call(xf, w1g, w1u, w1d, row(ln1_g), row(ln1_b), win)
    mix_params = [wlr.astype(BF16), row(gla_b_lr), row(gla_norm_g), mlstm_conv_w[0].astype(F32),
                  row(mlstm_conv_b), wqk.astype(BF16), gbias, row(mlstm_norm_g),
                  wo, row(ln2_g), row(ln2_b)]
    ffn_params = [w2g, w2u, w2d, row(ln3_g), row(ln3_b)]
    h3 = _mix_ffn_call(proj, h1, mix_params, ffn_params, seq)
    return h3.reshape(batch, seq, d)
```

```python
import functools

import jax
import jax.numpy as jnp
from jax import lax
from jax.experimental import pallas as pl
from jax.experimental.pallas import tpu as pltpu

F32 = jnp.float32
BF16 = jnp.bfloat16

D_MODEL = 1024
D_FF = 2816
HEADS = 4
PAIRS = HEADS // 2
DK = 64
DV = 128
KEY_W = HEADS * DK
VAL_W = HEADS * DV
LANES = 128
SUBLANES = 8
BF16_ROWS = 16
GLA_RANK = 16
GLA_TAU = 16.0
CONV_WIDTH = 4
CHUNK = 64
ALPHA = 2.0 ** 0.25
LN_EPS = 1e-5
HEAD_NORM_EPS = 1e-6
QK_SCALE = DK ** -0.5

Q0 = 0
K0 = Q0 + KEY_W
V0 = K0 + KEY_W
G0 = V0 + VAL_W
MU0 = G0 + VAL_W
MV0 = MU0 + VAL_W
MO0 = MV0 + VAL_W
SM0 = MO0 + VAL_W
SMALL_W = LANES
LR_LANE = 0
I_LANE = GLA_RANK
F_LANE = I_LANE + HEADS
PROJ_W = SM0 + SMALL_W

FFN_PROJ_ROWS = 512
FFN_SUB = 256
FFN_COLS = 512
PROJ_COLS = 1024
SUB_TILE_LAG = 0.15
MIX_ROWS = 256
CAST_STEPS = 8
VMEM_LIMIT = 60 * 1024 * 1024


def _layer_norm(z, g, b):
    mu = jnp.mean(z, axis=-1, keepdims=True)
    d = z - mu
    var = jnp.mean(d * d, axis=-1, keepdims=True)
    return d * lax.rsqrt(var + LN_EPS) * g + b


def _sigmoid(x):
    return 0.5 * jnp.tanh(0.5 * x) + 0.5


def _silu(x):
    h = 0.5 * x
    return h * jnp.tanh(h) + h


def _log_sigmoid(x):
    return jnp.minimum(x, 0.0) - jnp.log(1.0 + jnp.exp(-jnp.abs(x)))


def _dot(a, b):
    return jnp.dot(a, b, preferred_element_type=F32)


def _dot_nt(a, b):
    return lax.dot_general(a, b, (((1,), (1,)), ((), ())), preferred_element_type=F32)


def _dot_tn(a, b):
    return lax.dot_general(a, b, (((0,), (0,)), ((), ())), preferred_element_type=F32)


def _ffn_col_chunks():
    bounds = [min(c, D_FF) for c in range(0, D_FF + FFN_COLS, FFN_COLS)]
    return [slice(lo, hi) for lo, hi in zip(bounds[:-1], bounds[1:]) if hi > lo]


def _ffn_stages(load_x, wg_ref, wu_ref, wd_ref, lng_ref, lnb_ref, finish):
    xb = load_x().astype(BF16)
    col_chunks = _ffn_col_chunks()
    y = None
    gu = None
    for c in range(len(col_chunks) + 1):
        gu_next = None
        if c < len(col_chunks):
            cs = col_chunks[c]
            gu_next = (_dot(xb, wg_ref[:, cs]), _dot(xb, wu_ref[:, cs]))
        if gu is not None:
            part = _dot((_silu(gu[0]) * gu[1]).astype(BF16), wd_ref[col_chunks[c - 1], :])
            y = part if y is None else y + part
        gu = gu_next
        yield
    yield from finish(_layer_norm(ALPHA * load_x() + 0.5 * y, lng_ref[...], lnb_ref[...]))


def _n_ffn_stages():
    return len(_ffn_col_chunks()) + 1


def _run_interleaved(*streams):
    order = []
    for i, (_, n, start, end) in enumerate(streams):
        order += [(start + (k + 0.5) / n * (end - start), i) for k in range(n)]
    for _, i in sorted(order):
        next(streams[i][0], None)
    for gen, _, _, _ in streams:
        for _ in gen:
            pass


def _resident(shape):
    return pl.BlockSpec(shape, lambda *_: (0,) * len(shape), pipeline_mode=pl.Buffered(1))


def _ffn_proj_body(x_ref, wg_ref, wu_ref, wd_ref, lng_ref, lnb_ref, win_ref, h_ref, proj_ref):
    n_sub = x_ref.shape[0] // FFN_SUB
    proj_chunks = [slice(lo, min(lo + PROJ_COLS, PROJ_W)) for lo in range(0, PROJ_W, PROJ_COLS)]

    def make_stream(s):
        r = slice(s * FFN_SUB, (s + 1) * FFN_SUB)

        def finish(h):
            h_ref[r, :] = h
            hb = h.astype(BF16)
            for cs in proj_chunks:
                proj_ref[r, cs] = _dot(hb, win_ref[:, cs])
                yield

        return _ffn_stages(lambda: x_ref[r, :], wg_ref, wu_ref, wd_ref, lng_ref, lnb_ref, finish)

    n_stages = _n_ffn_stages() + len(proj_chunks)
    lag = SUB_TILE_LAG / max(n_sub - 1, 1)
    _run_interleaved(*[(make_stream(s), n_stages, s * lag, 1.0 - (n_sub - 1 - s) * lag)
                       for s in range(n_sub)])


def _ffn_proj_call(x, wg, wu, wd, lng, lnb, win):
    t = x.shape[0]
    row = lambda i: (i, 0)
    params = [wg, wu, wd, lng, lnb, win]
    return pl.pallas_call(
        _ffn_proj_body,
        grid=(t // FFN_PROJ_ROWS,),
        in_specs=[pl.BlockSpec((FFN_PROJ_ROWS, D_MODEL), row)] + [_resident(p.shape) for p in params],
        out_specs=[pl.BlockSpec((FFN_PROJ_ROWS, D_MODEL), row),
                   pl.BlockSpec((FFN_PROJ_ROWS, PROJ_W), row)],
        out_shape=[jax.ShapeDtypeStruct((t, D_MODEL), F32),
                   jax.ShapeDtypeStruct((t, PROJ_W), F32)],
        compiler_params=pltpu.CompilerParams(
            dimension_semantics=("arbitrary",), vmem_limit_bytes=VMEM_LIMIT),
        name="ffn_proj",
    )(x, *params)


def _chunk_scan(x, combine, fill):
    row_in_chunk = lax.broadcasted_iota(jnp.int32, x.shape, 0) & (CHUNK - 1)
    shift = 1
    while shift < CHUNK:
        x = combine(x, jnp.where(row_in_chunk >= shift, pltpu.roll(x, shift, axis=0), fill))
        shift *= 2
    return x


def _pair_block_diag(x):
    left = lax.broadcasted_iota(jnp.int32, x.shape, 1) < DK
    zero = jnp.zeros_like(x)
    return jnp.concatenate([jnp.where(left, x, zero), jnp.where(left, zero, x)], axis=0)


def _heads_to_lanes(x):
    left = lax.broadcasted_iota(jnp.int32, (x.shape[0], LANES), 1) < DK
    cols = [jnp.broadcast_to(x[:, F_LANE + h:F_LANE + h + 1], (x.shape[0], LANES))
            for h in range(HEADS)]
    return jnp.concatenate([jnp.where(left, cols[0], cols[1]),
                            jnp.where(left, cols[2], cols[3])], axis=1)


def _mixer_block(proj_ref, h1_ref, wlr_ref, blr_ref, gnorm_ref, convw_ref, convb_ref, wqk_ref,
                 gbias_ref, mnorm_ref, wout_ref, lng_ref, lnb_ref,
                 st_ref, c_ref, mw_ref, mg_ref, ext_ref, bc_ref, bcol_ref, acol_ref, amax_ref,
                 mqk_ref, ogla_ref, hm_ref, h2_ref, slot):
    rows_total = proj_ref.shape[0]
    n_chunks = rows_total // CHUNK

    small = proj_ref[:, SM0:SM0 + SMALL_W]
    z = _dot(small.astype(BF16), wlr_ref[...]) + blr_ref[...]
    bc_ref[...] = _chunk_scan(_log_sigmoid(z) * (1.0 / GLA_TAU), jnp.add, 0.0)
    yield

    gates = small + gbias_ref[...]
    b_gate = _chunk_scan(_log_sigmoid(gates), jnp.add, 0.0)
    a_gate = pltpu.roll(gates, F_LANE - I_LANE, axis=1) - b_gate
    bcol_ref[...] = b_gate
    acol_ref[...] = a_gate
    amax_ref[...] = _chunk_scan(a_gate, jnp.maximum, -jnp.inf)
    yield

    ext_ref[SUBLANES:, :] = proj_ref[:, MU0:MU0 + VAL_W]
    conv = ext_ref[SUBLANES:, :] * convw_ref[CONV_WIDTH - 1:CONV_WIDTH, :]
    for lag in range(1, CONV_WIDTH):
        conv = conv + (ext_ref[SUBLANES - lag:SUBLANES - lag + rows_total, :]
                       * convw_ref[CONV_WIDTH - 1 - lag:CONV_WIDTH - lag, :])
    ext_ref[0:SUBLANES, :] = ext_ref[rows_total:rows_total + SUBLANES, :]
    cact = _silu(conv + convb_ref[...]).astype(BF16)
    for p in range(PAIRS):
        pw = slice(2 * p * DV, (2 * p + 2) * DV)
        mqk_ref[:, pw] = _dot(cact[:, pw], wqk_ref[p])
    yield

    lane_w = lax.broadcasted_iota(jnp.int32, (CHUNK, KEY_W), 1) & (DK - 1)
    t_w = lax.broadcasted_iota(jnp.int32, (CHUNK, KEY_W), 0)
    tril_w = t_w >= lane_w
    diag_w = t_w == lane_w
    zero_v = jnp.zeros((CHUNK, DV), BF16)
    ones_v = jnp.ones((CHUNK, DV), BF16)

    def chunk_local(ci):
        rows = slice(ci * CHUNK, (ci + 1) * CHUNK)
        loc = {}
        q = proj_ref[rows, Q0:Q0 + KEY_W] * QK_SCALE
        k = proj_ref[rows, K0:K0 + KEY_W]
        b = bc_ref[rows, :]
        b_last = b[CHUNK - 1:CHUNK, :]
        b_mid = b[CHUNK // 2 - 1:CHUNK // 2, :]
        loc["qe"] = (q * jnp.exp(b)).astype(BF16)
        qr = (q * jnp.exp(b - b_mid)).astype(BF16)
        kr = (k * jnp.exp(b_mid - b)).astype(BF16)
        kd = (k * jnp.exp(b_last - b)).astype(BF16)
        loc["dec"] = jnp.exp(b_last)
        loc["gla"] = []
        for p in range(PAIRS):
            lp = slice(p * LANES, (p + 1) * LANES)
            va = proj_ref[rows, V0 + 2 * p * DV:V0 + (2 * p + 1) * DV].astype(BF16)
            vb = proj_ref[rows, V0 + (2 * p + 1) * DV:V0 + (2 * p + 2) * DV].astype(BF16)
            sc = _dot_nt(qr[:, lp], _pair_block_diag(kr[:, lp]))
            v_stack = jnp.concatenate([va, vb], axis=0)
            upd = _dot_tn(v_stack, _pair_block_diag(kd[:, lp]))
            loc["gla"].append((va, vb, sc, upd))
        a_w = _heads_to_lanes(acol_ref[rows, :])
        amax_w = _heads_to_lanes(amax_ref[rows, :])
        a_end = amax_w[CHUNK - 1:CHUNK, :]
        loc["amax_w"] = amax_w
        loc["a_end"] = a_end
        loc["a_row"] = jnp.sum(jnp.where(diag_w, a_w, 0.0), axis=0, keepdims=True)
        mq = jnp.concatenate([mqk_ref[rows, 2 * p * LANES:(2 * p + 1) * LANES]
                              for p in range(PAIRS)], axis=1) * QK_SCALE
        mk = jnp.concatenate([mqk_ref[rows, (2 * p + 1) * LANES:(2 * p + 2) * LANES]
                              for p in range(PAIRS)], axis=1)
        loc["mq"] = mq
        qb = mq.astype(BF16)
        kb = mk.astype(BF16)
        kw = (mk * jnp.exp(a_w - a_end)).astype(BF16)
        loc["mlstm"] = []
        for p in range(PAIRS):
            lp = slice(p * LANES, (p + 1) * LANES)
            va = proj_ref[rows, MV0 + 2 * p * DV:MV0 + (2 * p + 1) * DV].astype(BF16)
            vb = proj_ref[rows, MV0 + (2 * p + 1) * DV:MV0 + (2 * p + 2) * DV].astype(BF16)
            sqk = _dot_nt(qb[:, lp], _pair_block_diag(kb[:, lp]))
            v_stack = jnp.concatenate([va, vb], axis=0)
            upd_c = _dot_tn(v_stack, _pair_block_diag(kw[:, lp]))
            upd_n = jnp.sum(kw[:, lp].astype(F32), axis=0, keepdims=True)
            upd = jnp.concatenate([upd_c, jnp.broadcast_to(upd_n, (DV, LANES))], axis=0)
            loc["mlstm"].append((va, vb, sqk, upd))
        return loc

    def chunk_state(ci, loc):
        rows = slice(ci * CHUNK, (ci + 1) * CHUNK)
        for p in range(PAIRS):
            lp = slice(p * LANES, (p + 1) * LANES)
            va, vb, sc, upd = loc["gla"][p]
            sc = jnp.where(tril_w[:, lp], sc, 0.0).astype(BF16)
            v_bd = jnp.concatenate([jnp.concatenate([va, zero_v], axis=1),
                                    jnp.concatenate([zero_v, vb], axis=1)], axis=0)
            st = st_ref[p]
            out = _dot_nt(loc["qe"][:, lp], _pair_block_diag(st.astype(BF16))) + _dot(sc, v_bd)
            ogla_ref[rows, 2 * p * DV:(2 * p + 2) * DV] = out
            st_ref[p] = st * loc["dec"][:, lp] + upd

        m_prev_g = mg_ref[0:1, :]
        m_prev_w = mw_ref[0:1, :]
        mt_col = bcol_ref[rows, :] + jnp.maximum(m_prev_g, amax_ref[rows, :])
        big_m = jnp.maximum(m_prev_w, loc["amax_w"])
        e_mat = jnp.exp(jnp.where(tril_w, loc["a_row"] - big_m, -jnp.inf))
        m_end = big_m[CHUNK - 1:CHUNK, :]
        w_state = jnp.exp(m_prev_w - m_end)
        u_scale = jnp.exp(loc["a_end"] - m_end)
        qw = (loc["mq"] * jnp.exp(m_prev_w - big_m)).astype(BF16)
        for p in range(PAIRS):
            lp = slice(p * LANES, (p + 1) * LANES)
            va, vb, sqk, upd = loc["mlstm"][p]
            pm = (sqk * e_mat[:, lp]).astype(BF16)
            vaug_bd = jnp.concatenate(
                [jnp.concatenate([va, ones_v, zero_v, zero_v], axis=1),
                 jnp.concatenate([zero_v, zero_v, vb, ones_v], axis=1)], axis=0)
            ct = c_ref[p]
            nd = _dot_nt(qw[:, lp], _pair_block_diag(ct.astype(BF16))) + _dot(pm, vaug_bd)
            for j in range(2):
                h = 2 * p + j
                num = nd[:, 2 * j * DV:(2 * j + 1) * DV]
                den = nd[:, (2 * j + 1) * DV:(2 * j + 2) * DV]
                m_t = jnp.broadcast_to(mt_col[:, F_LANE + h:F_LANE + h + 1], (CHUNK, DV))
                hm_ref[rows, h * DV:(h + 1) * DV] = num / jnp.maximum(jnp.abs(den), jnp.exp(-m_t))
            c_ref[p] = ct * w_state[:, lp] + upd * u_scale[:, lp]
        m_new = mt_col[CHUNK - SUBLANES:CHUNK, :]
        mg_ref[...] = jnp.broadcast_to(m_new[SUBLANES - 1:, :], (SUBLANES, LANES))
        mw_ref[...] = jnp.broadcast_to(_heads_to_lanes(m_new)[SUBLANES - 1:, :],
                                       (SUBLANES, KEY_W))

    pending = None
    for ci in range(n_chunks + 1):
        nxt = chunk_local(ci) if ci < n_chunks else None
        if pending is not None:
            chunk_state(ci - 1, pending)
        pending = nxt
        yield

    parts = []
    for h in range(HEADS):
        vs = slice(h * DV, (h + 1) * DV)
        o = ogla_ref[:, vs]
        y = o * lax.rsqrt(jnp.mean(o * o, axis=-1, keepdims=True) + HEAD_NORM_EPS)
        y = y * gnorm_ref[:, vs]
        parts.append(y * _silu(proj_ref[:, G0 + h * DV:G0 + (h + 1) * DV]))
    yield
    for h in range(HEADS):
        vs = slice(h * DV, (h + 1) * DV)
        x = hm_ref[:, vs]
        d = x - jnp.mean(x, axis=-1, keepdims=True)
        var = jnp.mean(d * d, axis=-1, keepdims=True)
        y = d * lax.rsqrt(var + HEAD_NORM_EPS) * mnorm_ref[:, vs]
        parts.append(_sigmoid(proj_ref[:, MO0 + h * DV:MO0 + (h + 1) * DV]) * y)
    yield
    y = jnp.concatenate(parts, axis=1).astype(BF16)
    mix = _dot(y, wout_ref[...])
    h2_ref[slot] = _layer_norm(ALPHA * h1_ref[...] + mix, lng_ref[...], lnb_ref[...])


N_MIX_PARAMS = 11
N_FFN_PARAMS = 5


def _mix_ffn_body(*refs, blocks_per_seq):
    proj_ref, h1_ref = refs[0:2]
    mix_params = refs[2:2 + N_MIX_PARAMS]
    ffn_params = refs[2 + N_MIX_PARAMS:2 + N_MIX_PARAMS + N_FFN_PARAMS]
    out_ref = refs[2 + N_MIX_PARAMS + N_FFN_PARAMS]
    scratch = refs[3 + N_MIX_PARAMS + N_FFN_PARAMS:]
    hbuf_ref = scratch[0]
    mix_scratch = scratch[1:]
    st_ref, c_ref, mw_ref, mg_ref, ext_ref = mix_scratch[0:5]
    g = pl.program_id(0)
    slot = lax.rem(g, 2)

    @pl.when(g == 0)
    def _():
        hbuf_ref[1] = jnp.zeros(hbuf_ref.shape[1:], F32)

    @pl.when(lax.rem(g, blocks_per_seq) == 0)
    def _():
        st_ref[...] = jnp.zeros_like(st_ref)
        c_ref[...] = jnp.zeros_like(c_ref)
        mw_ref[...] = jnp.zeros_like(mw_ref)
        mg_ref[...] = jnp.zeros_like(mg_ref)
        ext_ref[0:SUBLANES, :] = jnp.zeros((SUBLANES, VAL_W), F32)

    def store_out(h):
        out_ref[...] = h
        yield

    n_mix_stages = proj_ref.shape[0] // CHUNK + 7
    _run_interleaved(
        (_mixer_block(proj_ref, h1_ref, *mix_params, *mix_scratch, hbuf_ref, slot),
         n_mix_stages, 0.0, 1.0),
        (_ffn_stages(lambda: hbuf_ref[1 - slot], *ffn_params, store_out),
         _n_ffn_stages() + 1, 0.0, 1.0))


def _mix_ffn_call(proj, h1, mix_params, ffn_params, seq):
    t = proj.shape[0]
    n_blocks = t // MIX_ROWS
    cur = lambda g: (jnp.minimum(g, n_blocks - 1), 0)
    prev = lambda g: (jnp.maximum(g - 1, 0), 0)
    assert len(mix_params) == N_MIX_PARAMS and len(ffn_params) == N_FFN_PARAMS
    in_specs = [pl.BlockSpec((MIX_ROWS, PROJ_W), cur), pl.BlockSpec((MIX_ROWS, D_MODEL), cur)]
    in_specs += [_resident(p.shape) for p in (*mix_params, *ffn_params)]
    scratch = [
        pltpu.VMEM((2, MIX_ROWS, D_MODEL), F32),
        pltpu.VMEM((PAIRS, DV, LANES), F32),
        pltpu.VMEM((PAIRS, 2 * DV, LANES), F32),
        pltpu.VMEM((SUBLANES, KEY_W), F32),
        pltpu.VMEM((SUBLANES, LANES), F32),
        pltpu.VMEM((MIX_ROWS + SUBLANES, VAL_W), F32),
        pltpu.VMEM((MIX_ROWS, KEY_W), F32),
        pltpu.VMEM((MIX_ROWS, SMALL_W), F32),
        pltpu.VMEM((MIX_ROWS, SMALL_W), F32),
        pltpu.VMEM((MIX_ROWS, SMALL_W), F32),
        pltpu.VMEM((MIX_ROWS, 2 * KEY_W), F32),
        pltpu.VMEM((MIX_ROWS, VAL_W), F32),
        pltpu.VMEM((MIX_ROWS, VAL_W), F32),
    ]
    return pl.pallas_call(
        functools.partial(_mix_ffn_body, blocks_per_seq=seq // MIX_ROWS),
        grid=(n_blocks + 1,),
        in_specs=in_specs,
        out_specs=pl.BlockSpec((MIX_ROWS, D_MODEL), prev),
        out_shape=jax.ShapeDtypeStruct((t, D_MODEL), F32),
        scratch_shapes=scratch,
        compiler_params=pltpu.CompilerParams(
            dimension_semantics=("arbitrary",), vmem_limit_bytes=VMEM_LIMIT),
        name="mix_ffn",
    )(proj, h1, *mix_params, *ffn_params)


def _prep_body(*refs):
    n_plain = (len(refs) - 2) // 2
    srcs, win_src = refs[:n_plain], refs[n_plain]
    dsts, win_dst = refs[n_plain + 1:-1], refs[-1]
    for src, dst in zip(srcs, dsts):
        dst[...] = src[...].astype(dst.dtype)
    head_w = G0 + VAL_W
    body_w = SM0 - MU0
    gate0 = head_w + GLA_RANK + body_w
    rows = win_dst.shape[0]
    win_dst[:, 0:head_w] = win_src[0, :, 0:head_w].astype(BF16)
    win_dst[:, MU0:SM0] = win_src[0, :, head_w + GLA_RANK:gate0].astype(BF16)
    win_dst[:, SM0:PROJ_W] = jnp.concatenate(
        [win_src[0, :, head_w:head_w + GLA_RANK], win_src[0, :, gate0:gate0 + 2 * HEADS],
         jnp.zeros((rows, SMALL_W - GLA_RANK - 2 * HEADS), F32)], axis=1).astype(BF16)


def _prep_weights(plain, w_in):
    def spec(rows, cols):
        assert rows % (BF16_ROWS * CAST_STEPS) == 0
        return pl.BlockSpec((rows // CAST_STEPS, cols), lambda i: (i, 0))
    _, rows, in_cols = w_in.shape
    win_spec = pl.BlockSpec((1, rows // CAST_STEPS, in_cols), lambda i: (0, i, 0))
    return pl.pallas_call(
        _prep_body,
        grid=(CAST_STEPS,),
        in_specs=[spec(*a.shape) for a in plain] + [win_spec],
        out_specs=[spec(*a.shape) for a in plain] + [spec(rows, PROJ_W)],
        out_shape=[jax.ShapeDtypeStruct(a.shape, BF16) for a in plain]
        + [jax.ShapeDtypeStruct((rows, PROJ_W), BF16)],
        compiler_params=pltpu.CompilerParams(
            dimension_semantics=("arbitrary",), vmem_limit_bytes=VMEM_LIMIT),
        name="prep_weights",
    )(*plain, w_in)


def _block_diag(w):
    h, dv, dk = w.shape
    eye = jnp.eye(h, dtype=w.dtype)
    return (w[:, :, None, :] * eye[:, None, :, None]).reshape(h * dv, h * dk)


def kernel(x, ln1_g, ln1_b, ffn1_w_gate, ffn1_w_up, ffn1_w_down, w_in, gla_w_lr, gla_b_lr,
           gla_norm_g, mlstm_conv_w, mlstm_conv_b, mlstm_w_q, mlstm_w_k, mlstm_b_i, mlstm_b_f,
           mlstm_norm_g, w_out, ln2_g, ln2_b, ffn2_w_gate, ffn2_w_up, ffn2_w_down, ln3_g, ln3_b):
    batch, seq, d = x.shape
    assert d == D_MODEL and seq % MIX_ROWS == 0 and (batch * seq) % FFN_PROJ_ROWS == 0
    assert ln1_g.shape[0] == 1, "single-layer problem"
    row = lambda v: v.reshape(1, -1).astype(F32)

    w1g, w1u, w1d, w2g, w2u, w2d, wo, win = _prep_weights(
        [ffn1_w_gate[0], ffn1_w_up[0], ffn1_w_down[0], ffn2_w_gate[0], ffn2_w_up[0],
         ffn2_w_down[0], w_out[0]], w_in)

    wlr = jnp.zeros((SMALL_W, KEY_W), F32).at[LR_LANE:LR_LANE + GLA_RANK].set(gla_w_lr[0])
    gbias = (jnp.zeros((1, SMALL_W), F32)
             .at[0, I_LANE:I_LANE + HEADS].set(mlstm_b_i[0])
             .at[0, F_LANE:F_LANE + HEADS].set(mlstm_b_f[0]))
    wqk = jnp.stack([
        jnp.concatenate([_block_diag(mlstm_w_q[0, 2 * p:2 * p + 2]),
                         _block_diag(mlstm_w_k[0, 2 * p:2 * p + 2])], axis=1)
        for p in range(PAIRS)])

    xf = x.reshape(batch * seq, d)
    h1, proj = _ffn_proj_call(xf, w1g, w1u, w1d, row(ln1_g), row(ln1_b), win)
    mix_params = [wlr.astype(BF16), row(gla_b_lr), row(gla_norm_g), mlstm_conv_w[0].astype(F32),
                  row(mlstm_conv_b), wqk.astype(BF16), gbias, row(mlstm_norm_g),
                  wo, row(ln2_g), row(ln2_b)]
    ffn_params = [w2g, w2u, w2d, row(ln3_g), row(ln3_b)]
    h3 = _mix_ffn_call(proj, h1, mix_params, ffn_params, seq)
    return h3.reshape(batch, seq, d)
```

```python
import functools

import jax
import jax.numpy as jnp
from jax import lax
from jax.experimental import pallas as pl
from jax.experimental.pallas import tpu as pltpu

F32 = jnp.float32
BF16 = jnp.bfloat16

D_MODEL = 1024
D_FF = 2816
HEADS = 4
PAIRS = HEADS // 2
DK = 64
DV = 128
KEY_W = HEADS * DK
VAL_W = HEADS * DV
LANES = 128
SUBLANES = 8
BF16_ROWS = 16
GLA_RANK = 16
GLA_TAU = 16.0
CONV_WIDTH = 4
CHUNK = 64
ALPHA = 2.0 ** 0.25
LN_EPS = 1e-5
HEAD_NORM_EPS = 1e-6
QK_SCALE = DK ** -0.5

Q0 = 0
K0 = Q0 + KEY_W
V0 = K0 + KEY_W
G0 = V0 + VAL_W
MU0 = G0 + VAL_W
MV0 = MU0 + VAL_W
MO0 = MV0 + VAL_W
SM0 = MO0 + VAL_W
SMALL_W = LANES
LR_LANE = 0
I_LANE = GLA_RANK
F_LANE = I_LANE + HEADS
PROJ_W = SM0 + SMALL_W

FFN_PROJ_ROWS = 512
FFN_SUB = 256
FFN_COLS = 512
PROJ_COLS = 1024
SUB_TILE_LAG = 0.15
MIX_ROWS = 256
CAST_STEPS = 8
VMEM_LIMIT = 60 * 1024 * 1024


def _layer_norm(z, g, b):
    mu = jnp.mean(z, axis=-1, keepdims=True)
    d = z - mu
    var = jnp.mean(d * d, axis=-1, keepdims=True)
    return d * lax.rsqrt(var + LN_EPS) * g + b


def _sigmoid(x):
    return 0.5 * jnp.tanh(0.5 * x) + 0.5


def _silu(x):
    h = 0.5 * x
    return h * jnp.tanh(h) + h


def _log_sigmoid(x):
    return jnp.minimum(x, 0.0) - jnp.log(1.0 + jnp.exp(-jnp.abs(x)))


def _dot(a, b):
    return jnp.dot(a, b, preferred_element_type=F32)


def _dot_nt(a, b):
    return lax.dot_general(a, b, (((1,), (1,)), ((), ())), preferred_element_type=F32)


def _dot_tn(a, b):
    return lax.dot_general(a, b, (((0,), (0,)), ((), ())), preferred_element_type=F32)


def _ffn_col_chunks():
    bounds = [min(c, D_FF) for c in range(0, D_FF + FFN_COLS, FFN_COLS)]
    return [slice(lo, hi) for lo, hi in zip(bounds[:-1], bounds[1:]) if hi > lo]


def _ffn_stages(load_x, wg_ref, wu_ref, wd_ref, lng_ref, lnb_ref, finish):
    xb = load_x().astype(BF16)
    col_chunks = _ffn_col_chunks()
    y = None
    gu = None
    for c in range(len(col_chunks) + 1):
        gu_next = None
        if c < len(col_chunks):
            cs = col_chunks[c]
            gu_next = (_dot(xb, wg_ref[:, cs]), _dot(xb, wu_ref[:, cs]))
        if gu is not None:
            part = _dot((_silu(gu[0]) * gu[1]).astype(BF16), wd_ref[col_chunks[c - 1], :])
            y = part if y is None else y + part
        gu = gu_next
        yield
    yield from finish(_layer_norm(ALPHA * load_x() + 0.5 * y, lng_ref[...], lnb_ref[...]))


def _n_ffn_stages():
    return len(_ffn_col_chunks()) + 1


def _run_interleaved(*streams):
    order = []
    for i, (_, n, start, end) in enumerate(streams):
        order += [(start + (k + 0.5) / n * (end - start), i) for k in range(n)]
    for _, i in sorted(order):
        next(streams[i][0], None)
    for gen, _, _, _ in streams:
        for _ in gen:
            pass


def _resident(shape):
    return pl.BlockSpec(shape, lambda *_: (0,) * len(shape), pipeline_mode=pl.Buffered(1))


def _ffn_proj_body(x_ref, wg_ref, wu_ref, wd_ref, lng_ref, lnb_ref, win_ref, h_ref, proj_ref):
    n_sub = x_ref.shape[0] // FFN_SUB
    proj_chunks = [slice(lo, min(lo + PROJ_COLS, PROJ_W)) for lo in range(0, PROJ_W, PROJ_COLS)]

    def make_stream(s):
        r = slice(s * FFN_SUB, (s + 1) * FFN_SUB)

        def finish(h):
            h_ref[r, :] = h
            hb = h.astype(BF16)
            for cs in proj_chunks:
                proj_ref[r, cs] = _dot(hb, win_ref[:, cs])
                yield

        return _ffn_stages(lambda: x_ref[r, :], wg_ref, wu_ref, wd_ref, lng_ref, lnb_ref, finish)

    n_stages = _n_ffn_stages() + len(proj_chunks)
    lag = SUB_TILE_LAG / max(n_sub - 1, 1)
    _run_interleaved(*[(make_stream(s), n_stages, s * lag, 1.0 - (n_sub - 1 - s) * lag)
                       for s in range(n_sub)])


def _ffn_proj_call(x, wg, wu, wd, lng, lnb, win):
    t = x.shape[0]
    row = lambda i: (i, 0)
    params = [wg, wu, wd, lng, lnb, win]
    return pl.pallas_call(
        _ffn_proj_body,
        grid=(t // FFN_PROJ_ROWS,),
        in_specs=[pl.BlockSpec((FFN_PROJ_ROWS, D_MODEL), row)] + [_resident(p.shape) for p in params],
        out_specs=[pl.BlockSpec((FFN_PROJ_ROWS, D_MODEL), row),
                   pl.BlockSpec((FFN_PROJ_ROWS, PROJ_W), row)],
        out_shape=[jax.ShapeDtypeStruct((t, D_MODEL), F32),
                   jax.ShapeDtypeStruct((t, PROJ_W), F32)],
        compiler_params=pltpu.CompilerParams(
            dimension_semantics=("arbitrary",), vmem_limit_bytes=VMEM_LIMIT),
        name="ffn_proj",
    )(x, *params)


def _chunk_scan(x, combine, fill):
    row_in_chunk = lax.broadcasted_iota(jnp.int32, x.shape, 0) & (CHUNK - 1)
    shift = 1
    while shift < CHUNK:
        x = combine(x, jnp.where(row_in_chunk >= shift, pltpu.roll(x, shift, axis=0), fill))
        shift *= 2
    return x


def _pair_block_diag(x):
    left = lax.broadcasted_iota(jnp.int32, x.shape, 1) < DK
    zero = jnp.zeros_like(x)
    return jnp.concatenate([jnp.where(left, x, zero), jnp.where(left, zero, x)], axis=0)


def _head_block_diag(x):
    lane_head = lax.broadcasted_iota(jnp.int32, x.shape, 1) // DK
    zero = jnp.zeros_like(x)
    return jnp.concatenate([jnp.where(lane_head == h, x, zero) for h in range(HEADS)], axis=0)


def _heads_to_lanes(x):
    left = lax.broadcasted_iota(jnp.int32, (x.shape[0], LANES), 1) < DK
    cols = [jnp.broadcast_to(x[:, F_LANE + h:F_LANE + h + 1], (x.shape[0], LANES))
            for h in range(HEADS)]
    return jnp.concatenate([jnp.where(left, cols[0], cols[1]),
                            jnp.where(left, cols[2], cols[3])], axis=1)


def _mixer_block(proj_ref, h1_ref, wlr_ref, blr_ref, gnorm_ref, convw_ref, convb_ref, wqk_ref,
                 gbias_ref, mnorm_ref, wout_ref, lng_ref, lnb_ref,
                 st_ref, c_ref, n_ref, mw_ref, mg_ref, ext_ref, bc_ref, bcol_ref, acol_ref,
                 amax_ref, mqk_ref, ogla_ref, hm_ref, h2_ref, slot):
    rows_total = proj_ref.shape[0]
    n_chunks = rows_total // CHUNK

    small = proj_ref[:, SM0:SM0 + SMALL_W]
    z = _dot(small.astype(BF16), wlr_ref[...]) + blr_ref[...]
    bc_ref[...] = _chunk_scan(_log_sigmoid(z) * (1.0 / GLA_TAU), jnp.add, 0.0)
    yield

    gates = small + gbias_ref[...]
    b_gate = _chunk_scan(_log_sigmoid(gates), jnp.add, 0.0)
    a_gate = pltpu.roll(gates, F_LANE - I_LANE, axis=1) - b_gate
    bcol_ref[...] = b_gate
    acol_ref[...] = a_gate
    amax_ref[...] = _chunk_scan(a_gate, jnp.maximum, -jnp.inf)
    yield

    ext_ref[SUBLANES:, :] = proj_ref[:, MU0:MU0 + VAL_W]
    conv = ext_ref[SUBLANES:, :] * convw_ref[CONV_WIDTH - 1:CONV_WIDTH, :]
    for lag in range(1, CONV_WIDTH):
        conv = conv + (ext_ref[SUBLANES - lag:SUBLANES - lag + rows_total, :]
                       * convw_ref[CONV_WIDTH - 1 - lag:CONV_WIDTH - lag, :])
    ext_ref[0:SUBLANES, :] = ext_ref[rows_total:rows_total + SUBLANES, :]
    cact = _silu(conv + convb_ref[...]).astype(BF16)
    for p in range(PAIRS):
        pw = slice(2 * p * DV, (2 * p + 2) * DV)
        mqk_ref[:, pw] = _dot(cact[:, pw], wqk_ref[p])
    yield

    lane_w = lax.broadcasted_iota(jnp.int32, (CHUNK, KEY_W), 1) & (DK - 1)
    t_w = lax.broadcasted_iota(jnp.int32, (CHUNK, KEY_W), 0)
    tril_w = t_w >= lane_w
    diag_w = t_w == lane_w
    zero_v = jnp.zeros((CHUNK, DV), BF16)

    def chunk_local(ci):
        rows = slice(ci * CHUNK, (ci + 1) * CHUNK)
        loc = {}
        q = proj_ref[rows, Q0:Q0 + KEY_W] * QK_SCALE
        k = proj_ref[rows, K0:K0 + KEY_W]
        b = bc_ref[rows, :]
        b_last = b[CHUNK - 1:CHUNK, :]
        b_mid = b[CHUNK // 2 - 1:CHUNK // 2, :]
        loc["qe"] = (q * jnp.exp(b)).astype(BF16)
        qr = (q * jnp.exp(b - b_mid)).astype(BF16)
        kr = (k * jnp.exp(b_mid - b)).astype(BF16)
        kd = (k * jnp.exp(b_last - b)).astype(BF16)
        loc["dec"] = jnp.exp(b_last)
        gv = [proj_ref[rows, V0 + h * DV:V0 + (h + 1) * DV].astype(BF16) for h in range(HEADS)]
        loc["gv"] = gv
        loc["sc"] = _dot_nt(qr, _head_block_diag(kr))
        loc["upd_s"] = _dot_tn(jnp.concatenate(gv, axis=0), _head_block_diag(kd))
        a_w = _heads_to_lanes(acol_ref[rows, :])
        amax_w = _heads_to_lanes(amax_ref[rows, :])
        a_end = amax_w[CHUNK - 1:CHUNK, :]
        loc["amax_w"] = amax_w
        loc["a_end"] = a_end
        loc["a_row"] = jnp.sum(jnp.where(diag_w, a_w, 0.0), axis=0, keepdims=True)
        mq = jnp.concatenate([mqk_ref[rows, 2 * p * LANES:(2 * p + 1) * LANES]
                              for p in range(PAIRS)], axis=1) * QK_SCALE
        mk = jnp.concatenate([mqk_ref[rows, (2 * p + 1) * LANES:(2 * p + 2) * LANES]
                              for p in range(PAIRS)], axis=1)
        loc["mq"] = mq
        kw = (mk * jnp.exp(a_w - a_end)).astype(BF16)
        mv = [proj_ref[rows, MV0 + h * DV:MV0 + (h + 1) * DV].astype(BF16) for h in range(HEADS)]
        loc["mv"] = mv
        loc["sqk"] = _dot_nt(mq.astype(BF16), _head_block_diag(mk.astype(BF16)))
        loc["upd_c"] = _dot_tn(jnp.concatenate(mv, axis=0), _head_block_diag(kw))
        loc["upd_n"] = jnp.sum(kw.astype(F32), axis=0, keepdims=True)
        return loc

    def chunk_state(ci, loc):
        rows = slice(ci * CHUNK, (ci + 1) * CHUNK)
        sc_all = jnp.where(tril_w, loc["sc"], 0.0).astype(BF16)
        for p in range(PAIRS):
            lp = slice(p * LANES, (p + 1) * LANES)
            va, vb = loc["gv"][2 * p], loc["gv"][2 * p + 1]
            v_bd = jnp.concatenate([jnp.concatenate([va, zero_v], axis=1),
                                    jnp.concatenate([zero_v, vb], axis=1)], axis=0)
            st = st_ref[:, lp]
            out = (_dot_nt(loc["qe"][:, lp], _pair_block_diag(st.astype(BF16)))
                   + _dot(sc_all[:, lp], v_bd))
            ogla_ref[rows, 2 * p * DV:(2 * p + 2) * DV] = out
            st_ref[:, lp] = st * loc["dec"][:, lp] + loc["upd_s"][:, lp]

        m_prev_g = mg_ref[0:1, :]
        m_prev_w = mw_ref[0:1, :]
        mt_col = bcol_ref[rows, :] + jnp.maximum(m_prev_g, amax_ref[rows, :])
        big_m = jnp.maximum(m_prev_w, loc["amax_w"])
        e_mat = jnp.exp(jnp.where(tril_w, loc["a_row"] - big_m, -jnp.inf))
        m_end = big_m[CHUNK - 1:CHUNK, :]
        w_state = jnp.exp(m_prev_w - m_end)
        u_scale = jnp.exp(loc["a_end"] - m_end)
        qw_f = loc["mq"] * jnp.exp(m_prev_w - big_m)
        qw = qw_f.astype(BF16)
        pm_f = loc["sqk"] * e_mat
        pm = pm_f.astype(BF16)
        n_prev = n_ref[0:1, :]
        den_terms = qw_f * n_prev + pm_f
        head_of_lane = lax.broadcasted_iota(jnp.int32, (CHUNK, LANES), 1) // DK
        for p in range(PAIRS):
            lp = slice(p * LANES, (p + 1) * LANES)
            va, vb = loc["mv"][2 * p], loc["mv"][2 * p + 1]
            v_bd = jnp.concatenate([jnp.concatenate([va, zero_v], axis=1),
                                    jnp.concatenate([zero_v, vb], axis=1)], axis=0)
            ct = c_ref[:, lp]
            num2 = _dot_nt(qw[:, lp], _pair_block_diag(ct.astype(BF16))) + _dot(pm[:, lp], v_bd)
            for j in range(2):
                h = 2 * p + j
                den = jnp.sum(jnp.where(head_of_lane == j, den_terms[:, lp], 0.0),
                              axis=1, keepdims=True)
                m_t = mt_col[:, F_LANE + h:F_LANE + h + 1]
                scale = jnp.maximum(jnp.abs(den), jnp.exp(-m_t))
                hm_ref[rows, h * DV:(h + 1) * DV] = num2[:, j * DV:(j + 1) * DV] / scale
            c_ref[:, lp] = ct * w_state[:, lp] + loc["upd_c"][:, lp] * u_scale[:, lp]
        n_ref[...] = jnp.broadcast_to(n_prev * w_state + loc["upd_n"] * u_scale,
                                      (SUBLANES, KEY_W))
        m_new = mt_col[CHUNK - SUBLANES:CHUNK, :]
        mg_ref[...] = jnp.broadcast_to(m_new[SUBLANES - 1:, :], (SUBLANES, LANES))
        mw_ref[...] = jnp.broadcast_to(_heads_to_lanes(m_new)[SUBLANES - 1:, :],
                                       (SUBLANES, KEY_W))

    pending = None
    for ci in range(n_chunks + 1):
        nxt = chunk_local(ci) if ci < n_chunks else None
        if pending is not None:
            chunk_state(ci - 1, pending)
        pending = nxt
        yield

    parts = []
    for h in range(HEADS):
        vs = slice(h * DV, (h + 1) * DV)
        o = ogla_ref[:, vs]
        y = o * lax.rsqrt(jnp.mean(o * o, axis=-1, keepdims=True) + HEAD_NORM_EPS)
        y = y * gnorm_ref[:, vs]
        parts.append(y * _silu(proj_ref[:, G0 + h * DV:G0 + (h + 1) * DV]))
    yield
    for h in range(HEADS):
        vs = slice(h * DV, (h + 1) * DV)
        x = hm_ref[:, vs]
        d = x - jnp.mean(x, axis=-1, keepdims=True)
        var = jnp.mean(d * d, axis=-1, keepdims=True)
        y = d * lax.rsqrt(var + HEAD_NORM_EPS) * mnorm_ref[:, vs]
        parts.append(_sigmoid(proj_ref[:, MO0 + h * DV:MO0 + (h + 1) * DV]) * y)
    yield
    y = jnp.concatenate(parts, axis=1).astype(BF16)
    mix = _dot(y, wout_ref[...])
    h2_ref[slot] = _layer_norm(ALPHA * h1_ref[...] + mix, lng_ref[...], lnb_ref[...])


N_MIX_PARAMS = 11
N_FFN_PARAMS = 5


def _mix_ffn_body(*refs, blocks_per_seq):
    proj_ref, h1_ref = refs[0:2]
    mix_params = refs[2:2 + N_MIX_PARAMS]
    ffn_params = refs[2 + N_MIX_PARAMS:2 + N_MIX_PARAMS + N_FFN_PARAMS]
    out_ref = refs[2 + N_MIX_PARAMS + N_FFN_PARAMS]
    scratch = refs[3 + N_MIX_PARAMS + N_FFN_PARAMS:]
    hbuf_ref = scratch[0]
    mix_scratch = scratch[1:]
    st_ref, c_ref, n_ref, mw_ref, mg_ref, ext_ref = mix_scratch[0:6]
    g = pl.program_id(0)
    slot = lax.rem(g, 2)

    @pl.when(g == 0)
    def _():
        hbuf_ref[1] = jnp.zeros(hbuf_ref.shape[1:], F32)

    @pl.when(lax.rem(g, blocks_per_seq) == 0)
    def _():
        st_ref[...] = jnp.zeros_like(st_ref)
        c_ref[...] = jnp.zeros_like(c_ref)
        n_ref[...] = jnp.zeros_like(n_ref)
        mw_ref[...] = jnp.zeros_like(mw_ref)
        mg_ref[...] = jnp.zeros_like(mg_ref)
        ext_ref[0:SUBLANES, :] = jnp.zeros((SUBLANES, VAL_W), F32)

    def store_out(h):
        out_ref[...] = h
        yield

    n_mix_stages = proj_ref.shape[0] // CHUNK + 7
    _run_interleaved(
        (_mixer_block(proj_ref, h1_ref, *mix_params, *mix_scratch, hbuf_ref, slot),
         n_mix_stages, 0.0, 1.0),
        (_ffn_stages(lambda: hbuf_ref[1 - slot], *ffn_params, store_out),
         _n_ffn_stages() + 1, 0.0, 1.0))


def _mix_ffn_call(proj, h1, mix_params, ffn_params, seq):
    t = proj.shape[0]
    n_blocks = t // MIX_ROWS
    cur = lambda g: (jnp.minimum(g, n_blocks - 1), 0)
    prev = lambda g: (jnp.maximum(g - 1, 0), 0)
    assert len(mix_params) == N_MIX_PARAMS and len(ffn_params) == N_FFN_PARAMS
    in_specs = [pl.BlockSpec((MIX_ROWS, PROJ_W), cur), pl.BlockSpec((MIX_ROWS, D_MODEL), cur)]
    in_specs += [_resident(p.shape) for p in (*mix_params, *ffn_params)]
    scratch = [
        pltpu.VMEM((2, MIX_ROWS, D_MODEL), F32),
        pltpu.VMEM((DV, KEY_W), F32),
        pltpu.VMEM((DV, KEY_W), F32),
        pltpu.VMEM((SUBLANES, KEY_W), F32),
        pltpu.VMEM((SUBLANES, KEY_W), F32),
        pltpu.VMEM((SUBLANES, LANES), F32),
        pltpu.VMEM((MIX_ROWS + SUBLANES, VAL_W), F32),
        pltpu.VMEM((MIX_ROWS, KEY_W), F32),
        pltpu.VMEM((MIX_ROWS, SMALL_W), F32),
        pltpu.VMEM((MIX_ROWS, SMALL_W), F32),
        pltpu.VMEM((MIX_ROWS, SMALL_W), F32),
        pltpu.VMEM((MIX_ROWS, 2 * KEY_W), F32),
        pltpu.VMEM((MIX_ROWS, VAL_W), F32),
        pltpu.VMEM((MIX_ROWS, VAL_W), F32),
    ]
    return pl.pallas_call(
        functools.partial(_mix_ffn_body, blocks_per_seq=seq // MIX_ROWS),
        grid=(n_blocks + 1,),
        in_specs=in_specs,
        out_specs=pl.BlockSpec((MIX_ROWS, D_MODEL), prev),
        out_shape=jax.ShapeDtypeStruct((t, D_MODEL), F32),
        scratch_shapes=scratch,
        compiler_params=pltpu.CompilerParams(
            dimension_semantics=("arbitrary",), vmem_limit_bytes=VMEM_LIMIT),
        name="mix_ffn",
    )(proj, h1, *mix_params, *ffn_params)


def _prep_body(*refs):
    n_plain = (len(refs) - 2) // 2
    srcs, win_src = refs[:n_plain], refs[n_plain]
    dsts, win_dst = refs[n_plain + 1:-1], refs[-1]
    for src, dst in zip(srcs, dsts):
        dst[...] = src[...].astype(dst.dtype)
    head_w = G0 + VAL_W
    body_w = SM0 - MU0
    gate0 = head_w + GLA_RANK + body_w
    rows = win_dst.shape[0]
    win_dst[:, 0:head_w] = win_src[0, :, 0:head_w].astype(BF16)
    win_dst[:, MU0:SM0] = win_src[0, :, head_w + GLA_RANK:gate0].astype(BF16)
    win_dst[:, SM0:PROJ_W] = jnp.concatenate(
        [win_src[0, :, head_w:head_w + GLA_RANK], win_src[0, :, gate0:gate0 + 2 * HEADS],
         jnp.zeros((rows, SMALL_W - GLA_RANK - 2 * HEADS), F32)], axis=1).astype(BF16)


def _prep_weights(plain, w_in):
    def spec(rows, cols):
        assert rows % (BF16_ROWS * CAST_STEPS) == 0
        return pl.BlockSpec((rows // CAST_STEPS, cols), lambda i: (i, 0))
    _, rows, in_cols = w_in.shape
    win_spec = pl.BlockSpec((1, rows // CAST_STEPS, in_cols), lambda i: (0, i, 0))
    return pl.pallas_call(
        _prep_body,
        grid=(CAST_STEPS,),
        in_specs=[spec(*a.shape) for a in plain] + [win_spec],
        out_specs=[spec(*a.shape) for a in plain] + [spec(rows, PROJ_W)],
        out_shape=[jax.ShapeDtypeStruct(a.shape, BF16) for a in plain]
        + [jax.ShapeDtypeStruct((rows, PROJ_W), BF16)],
        compiler_params=pltpu.CompilerParams(
            dimension_semantics=("arbitrary",), vmem_limit_bytes=VMEM_LIMIT),
        name="prep_weights",
    )(*plain, w_in)


def _block_diag(w):
    h, dv, dk = w.shape
    eye = jnp.eye(h, dtype=w.dtype)
    return (w[:, :, None, :] * eye[:, None, :, None]).reshape(h * dv, h * dk)


def kernel(x, ln1_g, ln1_b, ffn1_w_gate, ffn1_w_up, ffn1_w_down, w_in, gla_w_lr, gla_b_lr,
           gla_norm_g, mlstm_conv_w, mlstm_conv_b, mlstm_w_q, mlstm_w_k, mlstm_b_i, mlstm_b_f,
           mlstm_norm_g, w_out, ln2_g, ln2_b, ffn2_w_gate, ffn2_w_up, ffn2_w_down, ln3_g, ln3_b):
    batch, seq, d = x.shape
    assert d == D_MODEL and seq % MIX_ROWS == 0 and (batch * seq) % FFN_PROJ_ROWS == 0
    assert ln1_g.shape[0] == 1, "single-layer problem"
    row = lambda v: v.reshape(1, -1).astype(F32)

    w1g, w1u, w1d, w2g, w2u, w2d, wo, win = _prep_weights(
        [ffn1_w_gate[0], ffn1_w_up[0], ffn1_w_down[0], ffn2_w_gate[0], ffn2_w_up[0],
         ffn2_w_down[0], w_out[0]], w_in)

    wlr = jnp.zeros((SMALL_W, KEY_W), F32).at[LR_LANE:LR_LANE + GLA_RANK].set(gla_w_lr[0])
    gbias = (jnp.zeros((1, SMALL_W), F32)
             .at[0, I_LANE:I_LANE + HEADS].set(mlstm_b_i[0])
             .at[0, F_LANE:F_LANE + HEADS].set(mlstm_b_f[0]))
    wqk = jnp.stack([
        jnp.concatenate([_block_diag(mlstm_w_q[0, 2 * p:2 * p + 2]),
                         _block_diag(mlstm_w_k[0, 2 * p:2 * p + 2])], axis=1)
        for p in range(PAIRS)])

    xf = x.reshape(batch * seq, d)
    h1, proj = _ffn_proj_call(xf, w1g, w1u, w1d, row(ln1_g), row(ln1_b), win)
    mix_params = [wlr.astype(BF16), row(gla_b_lr), row(gla_norm_g), mlstm_conv_w[0].astype(F32),
                  row(mlstm_conv_b), wqk.astype(BF16), gbias, row(mlstm_norm_g),
                  wo, row(ln2_g), row(ln2_b)]
    ffn_params = [w2g, w2u, w2d, row(ln3_g), row(ln3_b)]
    h3 = _mix_ffn_call(proj, h1, mix_params, ffn_params, seq)
    return h3.reshape(batch, seq, d)
```

```python
import functools

import jax
import jax.numpy as jnp
from jax import lax
from jax.experimental import pallas as pl
from jax.experimental.pallas import tpu as pltpu

F32 = jnp.float32
BF16 = jnp.bfloat16

D_MODEL = 1024
D_FF = 2816
HEADS = 4
PAIRS = HEADS // 2
DK = 64
DV = 128
KEY_W = HEADS * DK
VAL_W = HEADS * DV
LANES = 128
SUBLANES = 8
BF16_ROWS = 16
GLA_RANK = 16
GLA_TAU = 16.0
CONV_WIDTH = 4
CHUNK = 64
ALPHA = 2.0 ** 0.25
LN_EPS = 1e-5
HEAD_NORM_EPS = 1e-6
QK_SCALE = DK ** -0.5

Q0 = 0
K0 = Q0 + KEY_W
V0 = K0 + KEY_W
G0 = V0 + VAL_W
MU0 = G0 + VAL_W
MV0 = MU0 + VAL_W
MO0 = MV0 + VAL_W
SM0 = MO0 + VAL_W
SMALL_W = LANES
LR_LANE = 0
I_LANE = GLA_RANK
F_LANE = I_LANE + HEADS
PROJ_W = SM0 + SMALL_W

FFN_PROJ_ROWS = 512
FFN_SUB = 256
FFN_COLS = 512
PROJ_COLS = 1024
SUB_TILE_LAG = 0.15
MIX_ROWS = 256
CAST_STEPS = 8
VMEM_LIMIT = 60 * 1024 * 1024


def _layer_norm(z, g, b):
    mu = jnp.mean(z, axis=-1, keepdims=True)
    d = z - mu
    var = jnp.mean(d * d, axis=-1, keepdims=True)
    return d * lax.rsqrt(var + LN_EPS) * g + b


def _sigmoid(x):
    return 0.5 * jnp.tanh(0.5 * x) + 0.5


def _silu(x):
    h = 0.5 * x
    return h * jnp.tanh(h) + h


def _log_sigmoid(x):
    return jnp.minimum(x, 0.0) - jnp.log(1.0 + jnp.exp(-jnp.abs(x)))


def _dot(a, b):
    return jnp.dot(a, b, preferred_element_type=F32)


def _dot_nt(a, b):
    return lax.dot_general(a, b, (((1,), (1,)), ((), ())), preferred_element_type=F32)


def _dot_tn(a, b):
    return lax.dot_general(a, b, (((0,), (0,)), ((), ())), preferred_element_type=F32)


def _ffn_col_chunks():
    bounds = [min(c, D_FF) for c in range(0, D_FF + FFN_COLS, FFN_COLS)]
    return [slice(lo, hi) for lo, hi in zip(bounds[:-1], bounds[1:]) if hi > lo]


def _ffn_stages(load_x, wg_ref, wu_ref, wd_ref, lng_ref, lnb_ref, finish):
    xb = load_x().astype(BF16)
    col_chunks = _ffn_col_chunks()
    y = None
    gu = None
    for c in range(len(col_chunks) + 1):
        gu_next = None
        if c < len(col_chunks):
            cs = col_chunks[c]
            gu_next = (_dot(xb, wg_ref[:, cs]), _dot(xb, wu_ref[:, cs]))
        if gu is not None:
            part = _dot((_silu(gu[0]) * gu[1]).astype(BF16), wd_ref[col_chunks[c - 1], :])
            y = part if y is None else y + part
        gu = gu_next
        yield
    yield from finish(_layer_norm(ALPHA * load_x() + 0.5 * y, lng_ref[...], lnb_ref[...]))


def _n_ffn_stages():
    return len(_ffn_col_chunks()) + 1


def _run_interleaved(*streams):
    order = []
    for i, (_, n, start, end) in enumerate(streams):
        order += [(start + (k + 0.5) / n * (end - start), i) for k in range(n)]
    for _, i in sorted(order):
        next(streams[i][0], None)
    for gen, _, _, _ in streams:
        for _ in gen:
            pass


def _resident(shape):
    return pl.BlockSpec(shape, lambda *_: (0,) * len(shape), pipeline_mode=pl.Buffered(1))


def _ffn_proj_body(x_ref, wg_ref, wu_ref, wd_ref, lng_ref, lnb_ref, win_ref, h_ref, proj_ref):
    n_sub = x_ref.shape[0] // FFN_SUB
    proj_chunks = [slice(lo, min(lo + PROJ_COLS, PROJ_W)) for lo in range(0, PROJ_W, PROJ_COLS)]

    def make_stream(s):
        r = slice(s * FFN_SUB, (s + 1) * FFN_SUB)

        def finish(h):
            h_ref[r, :] = h
            hb = h.astype(BF16)
            for cs in proj_chunks:
                proj_ref[r, cs] = _dot(hb, win_ref[:, cs])
                yield

        return _ffn_stages(lambda: x_ref[r, :], wg_ref, wu_ref, wd_ref, lng_ref, lnb_ref, finish)

    n_stages = _n_ffn_stages() + len(proj_chunks)
    lag = SUB_TILE_LAG / max(n_sub - 1, 1)
    _run_interleaved(*[(make_stream(s), n_stages, s * lag, 1.0 - (n_sub - 1 - s) * lag)
                       for s in range(n_sub)])


def _ffn_proj_call(x, wg, wu, wd, lng, lnb, win):
    t = x.shape[0]
    row = lambda i: (i, 0)
    params = [wg, wu, wd, lng, lnb, win]
    return pl.pallas_call(
        _ffn_proj_body,
        grid=(t // FFN_PROJ_ROWS,),
        in_specs=[pl.BlockSpec((FFN_PROJ_ROWS, D_MODEL), row)] + [_resident(p.shape) for p in params],
        out_specs=[pl.BlockSpec((FFN_PROJ_ROWS, D_MODEL), row),
                   pl.BlockSpec((FFN_PROJ_ROWS, PROJ_W), row)],
        out_shape=[jax.ShapeDtypeStruct((t, D_MODEL), F32),
                   jax.ShapeDtypeStruct((t, PROJ_W), F32)],
        compiler_params=pltpu.CompilerParams(
            dimension_semantics=("arbitrary",), vmem_limit_bytes=VMEM_LIMIT),
        name="ffn_proj",
    )(x, *params)


def _chunk_scan(x, combine, fill):
    row_in_chunk = lax.broadcasted_iota(jnp.int32, x.shape, 0) & (CHUNK - 1)
    shift = 1
    while shift < CHUNK:
        x = combine(x, jnp.where(row_in_chunk >= shift, pltpu.roll(x, shift, axis=0), fill))
        shift *= 2
    return x


def _pair_block_diag(x):
    left = lax.broadcasted_iota(jnp.int32, x.shape, 1) < DK
    zero = jnp.zeros_like(x)
    return jnp.concatenate([jnp.where(left, x, zero), jnp.where(left, zero, x)], axis=0)


def _head_block_diag(x):
    lane_head = lax.broadcasted_iota(jnp.int32, x.shape, 1) // DK
    zero = jnp.zeros_like(x)
    return jnp.concatenate([jnp.where(lane_head == h, x, zero) for h in range(HEADS)], axis=0)


def _heads_to_lanes(x):
    left = lax.broadcasted_iota(jnp.int32, (x.shape[0], LANES), 1) < DK
    cols = [jnp.broadcast_to(x[:, F_LANE + h:F_LANE + h + 1], (x.shape[0], LANES))
            for h in range(HEADS)]
    return jnp.concatenate([jnp.where(left, cols[0], cols[1]),
                            jnp.where(left, cols[2], cols[3])], axis=1)


def _mixer_block(proj_ref, h1_ref, wlr_ref, blr_ref, gnorm_ref, convw_ref, convb_ref, wqk_ref,
                 gbias_ref, mnorm_ref, wout_ref, lng_ref, lnb_ref,
                 st_ref, c_ref, n_ref, mw_ref, mg_ref, ext_ref, bc_ref, bcol_ref, acol_ref,
                 amax_ref, mqk_ref, ogla_ref, hm_ref, h2_ref, slot):
    rows_total = proj_ref.shape[0]
    n_chunks = rows_total // CHUNK

    small = proj_ref[:, SM0:SM0 + SMALL_W]
    z = _dot(small.astype(BF16), wlr_ref[...]) + blr_ref[...]
    bc_ref[...] = _chunk_scan(_log_sigmoid(z) * (1.0 / GLA_TAU), jnp.add, 0.0)
    yield

    gates = small + gbias_ref[...]
    b_gate = _chunk_scan(_log_sigmoid(gates), jnp.add, 0.0)
    a_gate = pltpu.roll(gates, F_LANE - I_LANE, axis=1) - b_gate
    bcol_ref[...] = b_gate
    acol_ref[...] = a_gate
    amax_ref[...] = _chunk_scan(a_gate, jnp.maximum, -jnp.inf)
    yield

    ext_ref[SUBLANES:, :] = proj_ref[:, MU0:MU0 + VAL_W]
    conv = ext_ref[SUBLANES:, :] * convw_ref[CONV_WIDTH - 1:CONV_WIDTH, :]
    for lag in range(1, CONV_WIDTH):
        conv = conv + (ext_ref[SUBLANES - lag:SUBLANES - lag + rows_total, :]
                       * convw_ref[CONV_WIDTH - 1 - lag:CONV_WIDTH - lag, :])
    ext_ref[0:SUBLANES, :] = ext_ref[rows_total:rows_total + SUBLANES, :]
    cact = _silu(conv + convb_ref[...]).astype(BF16)
    for p in range(PAIRS):
        pw = slice(2 * p * DV, (2 * p + 2) * DV)
        mqk_ref[:, pw] = _dot(cact[:, pw], wqk_ref[p])
    yield

    lane_w = lax.broadcasted_iota(jnp.int32, (CHUNK, KEY_W), 1) & (DK - 1)
    t_w = lax.broadcasted_iota(jnp.int32, (CHUNK, KEY_W), 0)
    tril_w = t_w >= lane_w
    diag_w = t_w == lane_w

    def chunk_local(ci):
        rows = slice(ci * CHUNK, (ci + 1) * CHUNK)
        loc = {}
        q = proj_ref[rows, Q0:Q0 + KEY_W] * QK_SCALE
        k = proj_ref[rows, K0:K0 + KEY_W]
        b = bc_ref[rows, :]
        b_last = b[CHUNK - 1:CHUNK, :]
        b_mid = b[CHUNK // 2 - 1:CHUNK // 2, :]
        loc["qe"] = (q * jnp.exp(b)).astype(BF16)
        qr = (q * jnp.exp(b - b_mid)).astype(BF16)
        kr = (k * jnp.exp(b_mid - b)).astype(BF16)
        kd = (k * jnp.exp(b_last - b)).astype(BF16)
        loc["dec"] = jnp.exp(b_last)
        gvt = jnp.concatenate([proj_ref[rows, V0 + h * DV:V0 + (h + 1) * DV] for h in range(HEADS)],
                              axis=0).T.astype(BF16)
        loc["gvt"] = gvt
        loc["sc"] = _dot_nt(qr, _head_block_diag(kr))
        loc["upd_s"] = _dot(gvt, _head_block_diag(kd))
        a_w = _heads_to_lanes(acol_ref[rows, :])
        amax_w = _heads_to_lanes(amax_ref[rows, :])
        a_end = amax_w[CHUNK - 1:CHUNK, :]
        loc["amax_w"] = amax_w
        loc["a_end"] = a_end
        loc["a_row"] = jnp.sum(jnp.where(diag_w, a_w, 0.0), axis=0, keepdims=True)
        mq = jnp.concatenate([mqk_ref[rows, 2 * p * LANES:(2 * p + 1) * LANES]
                              for p in range(PAIRS)], axis=1) * QK_SCALE
        mk = jnp.concatenate([mqk_ref[rows, (2 * p + 1) * LANES:(2 * p + 2) * LANES]
                              for p in range(PAIRS)], axis=1)
        loc["mq"] = mq
        kw = (mk * jnp.exp(a_w - a_end)).astype(BF16)
        mvt = jnp.concatenate([proj_ref[rows, MV0 + h * DV:MV0 + (h + 1) * DV] for h in range(HEADS)],
                              axis=0).T.astype(BF16)
        loc["mvt"] = mvt
        loc["sqk"] = _dot_nt(mq.astype(BF16), _head_block_diag(mk.astype(BF16)))
        loc["upd_c"] = _dot(mvt, _head_block_diag(kw))
        loc["upd_n"] = jnp.sum(kw.astype(F32), axis=0, keepdims=True)
        return loc

    def chunk_state(ci, loc):
        rows = slice(ci * CHUNK, (ci + 1) * CHUNK)
        sc_all = jnp.where(tril_w, loc["sc"], 0.0).astype(BF16)
        for p in range(PAIRS):
            lp = slice(p * LANES, (p + 1) * LANES)
            st = st_ref[:, lp]
            lhs = jnp.concatenate([loc["qe"][:, lp], sc_all[:, lp]], axis=1)
            rhs = jnp.concatenate([_pair_block_diag(st.astype(BF16)),
                                   _pair_block_diag(loc["gvt"][:, lp])], axis=1)
            ogla_ref[rows, 2 * p * DV:(2 * p + 2) * DV] = _dot_nt(lhs, rhs)
            st_ref[:, lp] = st * loc["dec"][:, lp] + loc["upd_s"][:, lp]

        m_prev_g = mg_ref[0:1, :]
        m_prev_w = mw_ref[0:1, :]
        mt_col = bcol_ref[rows, :] + jnp.maximum(m_prev_g, amax_ref[rows, :])
        big_m = jnp.maximum(m_prev_w, loc["amax_w"])
        e_mat = jnp.exp(jnp.where(tril_w, loc["a_row"] - big_m, -jnp.inf))
        m_end = big_m[CHUNK - 1:CHUNK, :]
        w_state = jnp.exp(m_prev_w - m_end)
        u_scale = jnp.exp(loc["a_end"] - m_end)
        qw_f = loc["mq"] * jnp.exp(m_prev_w - big_m)
        qw = qw_f.astype(BF16)
        pm_f = loc["sqk"] * e_mat
        pm = pm_f.astype(BF16)
        n_prev = n_ref[0:1, :]
        den_terms = qw_f * n_prev + pm_f
        head_of_lane = lax.broadcasted_iota(jnp.int32, (CHUNK, LANES), 1) // DK
        for p in range(PAIRS):
            lp = slice(p * LANES, (p + 1) * LANES)
            ct = c_ref[:, lp]
            lhs = jnp.concatenate([qw[:, lp], pm[:, lp]], axis=1)
            rhs = jnp.concatenate([_pair_block_diag(ct.astype(BF16)),
                                   _pair_block_diag(loc["mvt"][:, lp])], axis=1)
            num2 = _dot_nt(lhs, rhs)
            for j in range(2):
                h = 2 * p + j
                den = jnp.sum(jnp.where(head_of_lane == j, den_terms[:, lp], 0.0),
                              axis=1, keepdims=True)
                m_t = mt_col[:, F_LANE + h:F_LANE + h + 1]
                scale = jnp.maximum(jnp.abs(den), jnp.exp(-m_t))
                hm_ref[rows, h * DV:(h + 1) * DV] = num2[:, j * DV:(j + 1) * DV] / scale
            c_ref[:, lp] = ct * w_state[:, lp] + loc["upd_c"][:, lp] * u_scale[:, lp]
        n_ref[...] = jnp.broadcast_to(n_prev * w_state + loc["upd_n"] * u_scale,
                                      (SUBLANES, KEY_W))
        m_new = mt_col[CHUNK - SUBLANES:CHUNK, :]
        mg_ref[...] = jnp.broadcast_to(m_new[SUBLANES - 1:, :], (SUBLANES, LANES))
        mw_ref[...] = jnp.broadcast_to(_heads_to_lanes(m_new)[SUBLANES - 1:, :],
                                       (SUBLANES, KEY_W))

    pending = None
    for ci in range(n_chunks + 1):
        nxt = chunk_local(ci) if ci < n_chunks else None
        if pending is not None:
            chunk_state(ci - 1, pending)
        pending = nxt
        yield

    parts = []
    for h in range(HEADS):
        vs = slice(h * DV, (h + 1) * DV)
        o = ogla_ref[:, vs]
        y = o * lax.rsqrt(jnp.mean(o * o, axis=-1, keepdims=True) + HEAD_NORM_EPS)
        y = y * gnorm_ref[:, vs]
        parts.append(y * _silu(proj_ref[:, G0 + h * DV:G0 + (h + 1) * DV]))
    yield
    for h in range(HEADS):
        vs = slice(h * DV, (h + 1) * DV)
        x = hm_ref[:, vs]
        d = x - jnp.mean(x, axis=-1, keepdims=True)
        var = jnp.mean(d * d, axis=-1, keepdims=True)
        y = d * lax.rsqrt(var + HEAD_NORM_EPS) * mnorm_ref[:, vs]
        parts.append(_sigmoid(proj_ref[:, MO0 + h * DV:MO0 + (h + 1) * DV]) * y)
    yield
    y = jnp.concatenate(parts, axis=1).astype(BF16)
    mix = _dot(y, wout_ref[...])
    h2_ref[slot] = _layer_norm(ALPHA * h1_ref[...] + mix, lng_ref[...], lnb_ref[...])


N_MIX_PARAMS = 11
N_FFN_PARAMS = 5


def _mix_ffn_body(*refs, blocks_per_seq):
    proj_ref, h1_ref = refs[0:2]
    mix_params = refs[2:2 + N_MIX_PARAMS]
    ffn_params = refs[2 + N_MIX_PARAMS:2 + N_MIX_PARAMS + N_FFN_PARAMS]
    out_ref = refs[2 + N_MIX_PARAMS + N_FFN_PARAMS]
    scratch = refs[3 + N_MIX_PARAMS + N_FFN_PARAMS:]
    hbuf_ref = scratch[0]
    mix_scratch = scratch[1:]
    st_ref, c_ref, n_ref, mw_ref, mg_ref, ext_ref = mix_scratch[0:6]
    g = pl.program_id(0)
    slot = lax.rem(g, 2)

    @pl.when(g == 0)
    def _():
        hbuf_ref[1] = jnp.zeros(hbuf_ref.shape[1:], F32)

    @pl.when(lax.rem(g, blocks_per_seq) == 0)
    def _():
        st_ref[...] = jnp.zeros_like(st_ref)
        c_ref[...] = jnp.zeros_like(c_ref)
        n_ref[...] = jnp.zeros_like(n_ref)
        mw_ref[...] = jnp.zeros_like(mw_ref)
        mg_ref[...] = jnp.zeros_like(mg_ref)
        ext_ref[0:SUBLANES, :] = jnp.zeros((SUBLANES, VAL_W), F32)

    def store_out(h):
        out_ref[...] = h
        yield

    n_mix_stages = proj_ref.shape[0] // CHUNK + 7
    _run_interleaved(
        (_mixer_block(proj_ref, h1_ref, *mix_params, *mix_scratch, hbuf_ref, slot),
         n_mix_stages, 0.0, 1.0),
        (_ffn_stages(lambda: hbuf_ref[1 - slot], *ffn_params, store_out),
         _n_ffn_stages() + 1, 0.0, 1.0))


def _mix_ffn_call(proj, h1, mix_params, ffn_params, seq):
    t = proj.shape[0]
    n_blocks = t // MIX_ROWS
    cur = lambda g: (jnp.minimum(g, n_blocks - 1), 0)
    prev = lambda g: (jnp.maximum(g - 1, 0), 0)
    assert len(mix_params) == N_MIX_PARAMS and len(ffn_params) == N_FFN_PARAMS
    in_specs = [pl.BlockSpec((MIX_ROWS, PROJ_W), cur), pl.BlockSpec((MIX_ROWS, D_MODEL), cur)]
    in_specs += [_resident(p.shape) for p in (*mix_params, *ffn_params)]
    scratch = [
        pltpu.VMEM((2, MIX_ROWS, D_MODEL), F32),
        pltpu.VMEM((DV, KEY_W), F32),
        pltpu.VMEM((DV, KEY_W), F32),
        pltpu.VMEM((SUBLANES, KEY_W), F32),
        pltpu.VMEM((SUBLANES, KEY_W), F32),
        pltpu.VMEM((SUBLANES, LANES), F32),
        pltpu.VMEM((MIX_ROWS + SUBLANES, VAL_W), F32),
        pltpu.VMEM((MIX_ROWS, KEY_W), F32),
        pltpu.VMEM((MIX_ROWS, SMALL_W), F32),
        pltpu.VMEM((MIX_ROWS, SMALL_W), F32),
        pltpu.VMEM((MIX_ROWS, SMALL_W), F32),
        pltpu.VMEM((MIX_ROWS, 2 * KEY_W), F32),
        pltpu.VMEM((MIX_ROWS, VAL_W), F32),
        pltpu.VMEM((MIX_ROWS, VAL_W), F32),
    ]
    return pl.pallas_call(
        functools.partial(_mix_ffn_body, blocks_per_seq=seq // MIX_ROWS),
        grid=(n_blocks + 1,),
        in_specs=in_specs,
        out_specs=pl.BlockSpec((MIX_ROWS, D_MODEL), prev),
        out_shape=jax.ShapeDtypeStruct((t, D_MODEL), F32),
        scratch_shapes=scratch,
        compiler_params=pltpu.CompilerParams(
            dimension_semantics=("arbitrary",), vmem_limit_bytes=VMEM_LIMIT),
        name="mix_ffn",
    )(proj, h1, *mix_params, *ffn_params)


def _prep_body(*refs):
    n_plain = (len(refs) - 2) // 2
    srcs, win_src = refs[:n_plain], refs[n_plain]
    dsts, win_dst = refs[n_plain + 1:-1], refs[-1]
    for src, dst in zip(srcs, dsts):
        dst[...] = src[...].astype(dst.dtype)
    head_w = G0 + VAL_W
    body_w = SM0 - MU0
    gate0 = head_w + GLA_RANK + body_w
    rows = win_dst.shape[0]
    win_dst[:, 0:head_w] = win_src[0, :, 0:head_w].astype(BF16)
    win_dst[:, MU0:SM0] = win_src[0, :, head_w + GLA_RANK:gate0].astype(BF16)
    win_dst[:, SM0:PROJ_W] = jnp.concatenate(
        [win_src[0, :, head_w:head_w + GLA_RANK], win_src[0, :, gate0:gate0 + 2 * HEADS],
         jnp.zeros((rows, SMALL_W - GLA_RANK - 2 * HEADS), F32)], axis=1).astype(BF16)


def _prep_weights(plain, w_in):
    def spec(rows, cols):
        assert rows % (BF16_ROWS * CAST_STEPS) == 0
        return pl.BlockSpec((rows // CAST_STEPS, cols), lambda i: (i, 0))
    _, rows, in_cols = w_in.shape
    win_spec = pl.BlockSpec((1, rows // CAST_STEPS, in_cols), lambda i: (0, i, 0))
    return pl.pallas_call(
        _prep_body,
        grid=(CAST_STEPS,),
        in_specs=[spec(*a.shape) for a in plain] + [win_spec],
        out_specs=[spec(*a.shape) for a in plain] + [spec(rows, PROJ_W)],
        out_shape=[jax.ShapeDtypeStruct(a.shape, BF16) for a in plain]
        + [jax.ShapeDtypeStruct((rows, PROJ_W), BF16)],
        compiler_params=pltpu.CompilerParams(
            dimension_semantics=("arbitrary",), vmem_limit_bytes=VMEM_LIMIT),
        name="prep_weights",
    )(*plain, w_in)


def _block_diag(w):
    h, dv, dk = w.shape
    eye = jnp.eye(h, dtype=w.dtype)
    return (w[:, :, None, :] * eye[:, None, :, None]).reshape(h * dv, h * dk)


def kernel(x, ln1_g, ln1_b, ffn1_w_gate, ffn1_w_up, ffn1_w_down, w_in, gla_w_lr, gla_b_lr,
           gla_norm_g, mlstm_conv_w, mlstm_conv_b, mlstm_w_q, mlstm_w_k, mlstm_b_i, mlstm_b_f,
           mlstm_norm_g, w_out, ln2_g, ln2_b, ffn2_w_gate, ffn2_w_up, ffn2_w_down, ln3_g, ln3_b):
    batch, seq, d = x.shape
    assert d == D_MODEL and seq % MIX_ROWS == 0 and (batch * seq) % FFN_PROJ_ROWS == 0
    assert ln1_g.shape[0] == 1, "single-layer problem"
    row = lambda v: v.reshape(1, -1).astype(F32)

    w1g, w1u, w1d, w2g, w2u, w2d, wo, win = _prep_weights(
        [ffn1_w_gate[0], ffn1_w_up[0], ffn1_w_down[0], ffn2_w_gate[0], ffn2_w_up[0],
         ffn2_w_down[0], w_out[0]], w_in)

    wlr = jnp.zeros((SMALL_W, KEY_W), F32).at[LR_LANE:LR_LANE + GLA_RANK].set(gla_w_lr[0])
    gbias = (jnp.zeros((1, SMALL_W), F32)
             .at[0, I_LANE:I_LANE + HEADS].set(mlstm_b_i[0])
             .at[0, F_LANE:F_LANE + HEADS].set(mlstm_b_f[0]))
    wqk = jnp.stack([
        jnp.concatenate([_block_diag(mlstm_w_q[0, 2 * p:2 * p + 2]),
                         _block_diag(mlstm_w_k[0, 2 * p:2 * p + 2])], axis=1)
        for p in range(PAIRS)])

    xf = x.reshape(batch * seq, d)
    h1, proj = _ffn_proj_call(xf, w1g, w1u, w1d, row(ln1_g), row(ln1_b), win)
    mix_params = [wlr.astype(BF16), row(gla_b_lr), row(gla_norm_g), mlstm_conv_w[0].astype(F32),
                  row(mlstm_conv_b), wqk.astype(BF16), gbias, row(mlstm_norm_g),
                  wo, row(ln2_g), row(ln2_b)]
    ffn_params = [w2g, w2u, w2d, row(ln3_g), row(ln3_b)]
    h3 = _mix_ffn_call(proj, h1, mix_params, ffn_params, seq)
    return h3.reshape(batch, seq, d)
```

```python
import functools

import jax
import jax.numpy as jnp
from jax import lax
from jax.experimental import pallas as pl
from jax.experimental.pallas import tpu as pltpu

F32 = jnp.float32
BF16 = jnp.bfloat16

D_MODEL = 1024
D_FF = 2816
HEADS = 4
PAIRS = HEADS // 2
DK = 64
DV = 128
KEY_W = HEADS * DK
VAL_W = HEADS * DV
LANES = 128
SUBLANES = 8
BF16_ROWS = 16
GLA_RANK = 16
GLA_TAU = 16.0
CONV_WIDTH = 4
CHUNK = 64
ALPHA = 2.0 ** 0.25
LN_EPS = 1e-5
HEAD_NORM_EPS = 1e-6
QK_SCALE = DK ** -0.5

Q0 = 0
K0 = Q0 + KEY_W
V0 = K0 + KEY_W
G0 = V0 + VAL_W
MU0 = G0 + VAL_W
MV0 = MU0 + VAL_W
MO0 = MV0 + VAL_W
SM0 = MO0 + VAL_W
SMALL_W = LANES
LR_LANE = 0
I_LANE = GLA_RANK
F_LANE = I_LANE + HEADS
PROJ_W = SM0 + SMALL_W

FFN_PROJ_ROWS = 512
FFN_SUB = 256
FFN_COLS = 512
PROJ_COLS = 1024
SUB_TILE_LAG = 0.15
MIX_ROWS = 256
CAST_STEPS = 8
VMEM_LIMIT = 60 * 1024 * 1024


def _deepnorm(x, y, y_scale, g, b):
    z = x + (y_scale / ALPHA) * y
    mu = jnp.mean(z, axis=-1, keepdims=True)
    d = z - mu
    var = jnp.mean(d * d, axis=-1, keepdims=True)
    return d * lax.rsqrt(var + LN_EPS / (ALPHA * ALPHA)) * g + b


def _sigmoid(x):
    return 0.5 * jnp.tanh(0.5 * x) + 0.5


def _silu(x):
    h = 0.5 * x
    return h * jnp.tanh(h) + h


def _log_sigmoid(x):
    return jnp.minimum(x, 0.0) - jnp.log(1.0 + jnp.exp(-jnp.abs(x)))


def _dot(a, b):
    return jnp.dot(a, b, preferred_element_type=F32)


def _dot_nt(a, b):
    return lax.dot_general(a, b, (((1,), (1,)), ((), ())), preferred_element_type=F32)


def _dot_tn(a, b):
    return lax.dot_general(a, b, (((0,), (0,)), ((), ())), preferred_element_type=F32)


def _ffn_col_chunks():
    bounds = [min(c, D_FF) for c in range(0, D_FF + FFN_COLS, FFN_COLS)]
    return [slice(lo, hi) for lo, hi in zip(bounds[:-1], bounds[1:]) if hi > lo]


def _ffn_stages(load_x, wg_ref, wu_ref, wd_ref, lng_ref, lnb_ref, finish):
    col_chunks = _ffn_col_chunks()
    y = None
    gu = None
    for c in range(len(col_chunks) + 1):
        gu_next = None
        if c < len(col_chunks):
            cs = col_chunks[c]
            xb = load_x().astype(BF16)
            gu_next = (_dot(xb, wg_ref[:, cs]), _dot(xb, wu_ref[:, cs]))
        if gu is not None:
            part = _dot((_silu(gu[0]) * gu[1]).astype(BF16), wd_ref[col_chunks[c - 1], :])
            y = part if y is None else y + part
        gu = gu_next
        yield
    yield from finish(_deepnorm(load_x(), y, 0.5, lng_ref[...], lnb_ref[...]))


def _n_ffn_stages():
    return len(_ffn_col_chunks()) + 1


def _run_interleaved(*streams):
    order = []
    for i, (_, n, start, end) in enumerate(streams):
        order += [(start + (k + 0.5) / n * (end - start), i) for k in range(n)]
    for _, i in sorted(order):
        next(streams[i][0], None)
    for gen, _, _, _ in streams:
        for _ in gen:
            pass


def _resident(shape):
    return pl.BlockSpec(shape, lambda *_: (0,) * len(shape), pipeline_mode=pl.Buffered(1))


def _ffn_proj_body(x_ref, wg_ref, wu_ref, wd_ref, lng_ref, lnb_ref, win_ref, h_ref, proj_ref):
    n_sub = x_ref.shape[0] // FFN_SUB
    proj_chunks = [slice(lo, min(lo + PROJ_COLS, PROJ_W)) for lo in range(0, PROJ_W, PROJ_COLS)]

    def make_stream(s):
        r = slice(s * FFN_SUB, (s + 1) * FFN_SUB)

        def finish(h):
            h_ref[r, :] = h
            hb = h.astype(BF16)
            for cs in proj_chunks:
                proj_ref[r, cs] = _dot(hb, win_ref[:, cs])
                yield

        return _ffn_stages(lambda: x_ref[r, :], wg_ref, wu_ref, wd_ref, lng_ref, lnb_ref, finish)

    n_stages = _n_ffn_stages() + len(proj_chunks)
    lag = SUB_TILE_LAG / max(n_sub - 1, 1)
    _run_interleaved(*[(make_stream(s), n_stages, s * lag, 1.0 - (n_sub - 1 - s) * lag)
                       for s in range(n_sub)])


def _ffn_proj_call(x, wg, wu, wd, lng, lnb, win):
    t = x.shape[0]
    row = lambda i: (i, 0)
    params = [wg, wu, wd, lng, lnb, win]
    return pl.pallas_call(
        _ffn_proj_body,
        grid=(t // FFN_PROJ_ROWS,),
        in_specs=[pl.BlockSpec((FFN_PROJ_ROWS, D_MODEL), row)] + [_resident(p.shape) for p in params],
        out_specs=[pl.BlockSpec((FFN_PROJ_ROWS, D_MODEL), row),
                   pl.BlockSpec((FFN_PROJ_ROWS, PROJ_W), row)],
        out_shape=[jax.ShapeDtypeStruct((t, D_MODEL), F32),
                   jax.ShapeDtypeStruct((t, PROJ_W), F32)],
        compiler_params=pltpu.CompilerParams(
            dimension_semantics=("arbitrary",), vmem_limit_bytes=VMEM_LIMIT),
        name="ffn_proj",
    )(x, *params)


def _chunk_scan(x, combine, fill):
    row_in_chunk = lax.broadcasted_iota(jnp.int32, x.shape, 0) & (CHUNK - 1)
    shift = 1
    while shift < CHUNK:
        x = combine(x, jnp.where(row_in_chunk >= shift, pltpu.roll(x, shift, axis=0), fill))
        shift *= 2
    return x


def _pair_block_diag(x):
    left = lax.broadcasted_iota(jnp.int32, x.shape, 1) < DK
    zero = jnp.zeros_like(x)
    return jnp.concatenate([jnp.where(left, x, zero), jnp.where(left, zero, x)], axis=0)


def _head_block_diag(x):
    lane_head = lax.broadcasted_iota(jnp.int32, x.shape, 1) // DK
    zero = jnp.zeros_like(x)
    return jnp.concatenate([jnp.where(lane_head == h, x, zero) for h in range(HEADS)], axis=0)


def _heads_to_lanes(x):
    left = lax.broadcasted_iota(jnp.int32, (x.shape[0], LANES), 1) < DK
    cols = [jnp.broadcast_to(x[:, F_LANE + h:F_LANE + h + 1], (x.shape[0], LANES))
            for h in range(HEADS)]
    return jnp.concatenate([jnp.where(left, cols[0], cols[1]),
                            jnp.where(left, cols[2], cols[3])], axis=1)


def _mixer_block(proj_ref, h1_ref, wlr_ref, blr_ref, gnorm_ref, convw_ref, convb_ref, wqk_ref,
                 gbias_ref, mnorm_ref, wout_ref, lng_ref, lnb_ref,
                 st_ref, c_ref, n_ref, mw_ref, mg_ref, ext_ref, bc_ref, bcol_ref, acol_ref,
                 amax_ref, mqk_ref, ogla_ref, hm_ref, h2_ref, slot):
    rows_total = proj_ref.shape[0]
    n_chunks = rows_total // CHUNK

    small = proj_ref[:, SM0:SM0 + SMALL_W]
    z = _dot(small.astype(BF16), wlr_ref[...]) + blr_ref[...]
    bc_ref[...] = _chunk_scan(_log_sigmoid(z) * (1.0 / GLA_TAU), jnp.add, 0.0)
    yield

    gates = small + gbias_ref[...]
    b_gate = _chunk_scan(_log_sigmoid(gates), jnp.add, 0.0)
    a_gate = pltpu.roll(gates, F_LANE - I_LANE, axis=1) - b_gate
    bcol_ref[...] = b_gate
    acol_ref[...] = a_gate
    amax_ref[...] = _chunk_scan(a_gate, jnp.maximum, -jnp.inf)
    yield

    ext_ref[SUBLANES:, :] = proj_ref[:, MU0:MU0 + VAL_W]
    conv = ext_ref[SUBLANES:, :] * convw_ref[CONV_WIDTH - 1:CONV_WIDTH, :]
    for lag in range(1, CONV_WIDTH):
        conv = conv + (ext_ref[SUBLANES - lag:SUBLANES - lag + rows_total, :]
                       * convw_ref[CONV_WIDTH - 1 - lag:CONV_WIDTH - lag, :])
    ext_ref[0:SUBLANES, :] = ext_ref[rows_total:rows_total + SUBLANES, :]
    cact = _silu(conv + convb_ref[...]).astype(BF16)
    for p in range(PAIRS):
        pw = slice(2 * p * DV, (2 * p + 2) * DV)
        mqk_ref[:, pw] = _dot(cact[:, pw], wqk_ref[p])
    yield

    lane_w = lax.broadcasted_iota(jnp.int32, (CHUNK, KEY_W), 1) & (DK - 1)
    t_w = lax.broadcasted_iota(jnp.int32, (CHUNK, KEY_W), 0)
    tril_w = t_w >= lane_w
    diag_w = t_w == lane_w
    zero_v = jnp.zeros((CHUNK, DV), BF16)

    def chunk_local(ci):
        rows = slice(ci * CHUNK, (ci + 1) * CHUNK)
        loc = {}
        q = proj_ref[rows, Q0:Q0 + KEY_W] * QK_SCALE
        k = proj_ref[rows, K0:K0 + KEY_W]
        b = bc_ref[rows, :]
        b_last = b[CHUNK - 1:CHUNK, :]
        b_mid = b[CHUNK // 2 - 1:CHUNK // 2, :]
        loc["qe"] = (q * jnp.exp(b)).astype(BF16)
        qr = (q * jnp.exp(b - b_mid)).astype(BF16)
        kr = (k * jnp.exp(b_mid - b)).astype(BF16)
        kd = (k * jnp.exp(b_last - b)).astype(BF16)
        loc["dec"] = jnp.exp(b_last)
        gv = [proj_ref[rows, V0 + h * DV:V0 + (h + 1) * DV].astype(BF16) for h in range(HEADS)]
        loc["gv"] = gv
        loc["sc"] = _dot_nt(qr, _head_block_diag(kr))
        loc["upd_s"] = _dot_tn(jnp.concatenate(gv, axis=0), _head_block_diag(kd))
        a_w = _heads_to_lanes(acol_ref[rows, :])
        amax_w = _heads_to_lanes(amax_ref[rows, :])
        a_end = amax_w[CHUNK - 1:CHUNK, :]
        loc["amax_w"] = amax_w
        loc["a_end"] = a_end
        loc["a_row"] = jnp.sum(jnp.where(diag_w, a_w, 0.0), axis=0, keepdims=True)
        mq = jnp.concatenate([mqk_ref[rows, 2 * p * LANES:(2 * p + 1) * LANES]
                              for p in range(PAIRS)], axis=1) * QK_SCALE
        mk = jnp.concatenate([mqk_ref[rows, (2 * p + 1) * LANES:(2 * p + 2) * LANES]
                              for p in range(PAIRS)], axis=1)
        loc["mq"] = mq
        kw = (mk * jnp.exp(a_w - a_end)).astype(BF16)
        mv = [proj_ref[rows, MV0 + h * DV:MV0 + (h + 1) * DV].astype(BF16) for h in range(HEADS)]
        loc["mv"] = mv
        loc["sqk"] = _dot_nt(mq.astype(BF16), _head_block_diag(mk.astype(BF16)))
        loc["upd_c"] = _dot_tn(jnp.concatenate(mv, axis=0), _head_block_diag(kw))
        loc["upd_n"] = jnp.sum(kw.astype(F32), axis=0, keepdims=True)
        return loc

    def chunk_state(ci, loc):
        rows = slice(ci * CHUNK, (ci + 1) * CHUNK)
        sc_all = jnp.where(tril_w, loc["sc"], 0.0).astype(BF16)
        for p in range(PAIRS):
            lp = slice(p * LANES, (p + 1) * LANES)
            va, vb = loc["gv"][2 * p], loc["gv"][2 * p + 1]
            v_bd = jnp.concatenate([jnp.concatenate([va, zero_v], axis=1),
                                    jnp.concatenate([zero_v, vb], axis=1)], axis=0)
            st = st_ref[:, lp]
            out = (_dot_nt(loc["qe"][:, lp], _pair_block_diag(st.astype(BF16)))
                   + _dot(sc_all[:, lp], v_bd))
            ogla_ref[rows, 2 * p * DV:(2 * p + 2) * DV] = out
            st_ref[:, lp] = st * loc["dec"][:, lp] + loc["upd_s"][:, lp]

        m_prev_g = mg_ref[0:1, :]
        m_prev_w = mw_ref[0:1, :]
        mt_col = bcol_ref[rows, :] + jnp.maximum(m_prev_g, amax_ref[rows, :])
        big_m = jnp.maximum(m_prev_w, loc["amax_w"])
        e_mat = jnp.exp(jnp.where(tril_w, loc["a_row"] - big_m, -jnp.inf))
        m_end = big_m[CHUNK - 1:CHUNK, :]
        w_state = jnp.exp(m_prev_w - m_end)
        u_scale = jnp.exp(loc["a_end"] - m_end)
        qw_f = loc["mq"] * jnp.exp(m_prev_w - big_m)
        qw = qw_f.astype(BF16)
        pm_f = loc["sqk"] * e_mat
        pm = pm_f.astype(BF16)
        n_prev = n_ref[0:1, :]
        den_terms = qw_f * n_prev + pm_f
        head_of_lane = lax.broadcasted_iota(jnp.int32, (CHUNK, LANES), 1) // DK
        for p in range(PAIRS):
            lp = slice(p * LANES, (p + 1) * LANES)
            va, vb = loc["mv"][2 * p], loc["mv"][2 * p + 1]
            v_bd = jnp.concatenate([jnp.concatenate([va, zero_v], axis=1),
                                    jnp.concatenate([zero_v, vb], axis=1)], axis=0)
            ct = c_ref[:, lp]
            num2 = _dot_nt(qw[:, lp], _pair_block_diag(ct.astype(BF16))) + _dot(pm[:, lp], v_bd)
            for j in range(2):
                h = 2 * p + j
                den = jnp.sum(jnp.where(head_of_lane == j, den_terms[:, lp], 0.0),
                              axis=1, keepdims=True)
                m_t = mt_col[:, F_LANE + h:F_LANE + h + 1]
                scale = jnp.maximum(jnp.abs(den), jnp.exp(-m_t))
                hm_ref[rows, h * DV:(h + 1) * DV] = num2[:, j * DV:(j + 1) * DV] / scale
            c_ref[:, lp] = ct * w_state[:, lp] + loc["upd_c"][:, lp] * u_scale[:, lp]
        n_ref[...] = jnp.broadcast_to(n_prev * w_state + loc["upd_n"] * u_scale,
                                      (SUBLANES, KEY_W))
        m_new = mt_col[CHUNK - SUBLANES:CHUNK, :]
        mg_ref[...] = jnp.broadcast_to(m_new[SUBLANES - 1:, :], (SUBLANES, LANES))
        mw_ref[...] = jnp.broadcast_to(_heads_to_lanes(m_new)[SUBLANES - 1:, :],
                                       (SUBLANES, KEY_W))

    pending = None
    for ci in range(n_chunks + 1):
        nxt = chunk_local(ci) if ci < n_chunks else None
        if pending is not None:
            chunk_state(ci - 1, pending)
        pending = nxt
        yield

    parts = []
    for h in range(HEADS):
        vs = slice(h * DV, (h + 1) * DV)
        o = ogla_ref[:, vs]
        y = o * lax.rsqrt(jnp.mean(o * o, axis=-1, keepdims=True) + HEAD_NORM_EPS)
        y = y * gnorm_ref[:, vs]
        parts.append(y * _silu(proj_ref[:, G0 + h * DV:G0 + (h + 1) * DV]))
    yield
    for h in range(HEADS):
        vs = slice(h * DV, (h + 1) * DV)
        x = hm_ref[:, vs]
        d = x - jnp.mean(x, axis=-1, keepdims=True)
        var = jnp.mean(d * d, axis=-1, keepdims=True)
        y = d * lax.rsqrt(var + HEAD_NORM_EPS) * mnorm_ref[:, vs]
        parts.append(_sigmoid(proj_ref[:, MO0 + h * DV:MO0 + (h + 1) * DV]) * y)
    yield
    y = jnp.concatenate(parts, axis=1).astype(BF16)
    mix = _dot(y, wout_ref[...])
    h2_ref[slot] = _deepnorm(h1_ref[...], mix, 1.0, lng_ref[...], lnb_ref[...])


N_MIX_PARAMS = 11
N_FFN_PARAMS = 5


def _mix_ffn_body(*refs, blocks_per_seq):
    proj_ref, h1_ref = refs[0:2]
    mix_params = refs[2:2 + N_MIX_PARAMS]
    ffn_params = refs[2 + N_MIX_PARAMS:2 + N_MIX_PARAMS + N_FFN_PARAMS]
    out_ref = refs[2 + N_MIX_PARAMS + N_FFN_PARAMS]
    scratch = refs[3 + N_MIX_PARAMS + N_FFN_PARAMS:]
    hbuf_ref = scratch[0]
    mix_scratch = scratch[1:]
    st_ref, c_ref, n_ref, mw_ref, mg_ref, ext_ref = mix_scratch[0:6]
    g = pl.program_id(0)
    slot = lax.rem(g, 2)

    @pl.when(g == 0)
    def _():
        hbuf_ref[1] = jnp.zeros(hbuf_ref.shape[1:], F32)

    @pl.when(lax.rem(g, blocks_per_seq) == 0)
    def _():
        st_ref[...] = jnp.zeros_like(st_ref)
        c_ref[...] = jnp.zeros_like(c_ref)
        n_ref[...] = jnp.zeros_like(n_ref)
        mw_ref[...] = jnp.zeros_like(mw_ref)
        mg_ref[...] = jnp.zeros_like(mg_ref)
        ext_ref[0:SUBLANES, :] = jnp.zeros((SUBLANES, VAL_W), F32)

    def store_out(h):
        out_ref[...] = h
        yield

    n_mix_stages = proj_ref.shape[0] // CHUNK + 7
    _run_interleaved(
        (_mixer_block(proj_ref, h1_ref, *mix_params, *mix_scratch, hbuf_ref, slot),
         n_mix_stages, 0.0, 1.0),
        (_ffn_stages(lambda: hbuf_ref[1 - slot], *ffn_params, store_out),
         _n_ffn_stages() + 1, 0.0, 1.0))


def _mix_ffn_call(proj, h1, mix_params, ffn_params, seq):
    t = proj.shape[0]
    n_blocks = t // MIX_ROWS
    cur = lambda g: (jnp.minimum(g, n_blocks - 1), 0)
    prev = lambda g: (jnp.maximum(g - 1, 0), 0)
    assert len(mix_params) == N_MIX_PARAMS and len(ffn_params) == N_FFN_PARAMS
    in_specs = [pl.BlockSpec((MIX_ROWS, PROJ_W), cur), pl.BlockSpec((MIX_ROWS, D_MODEL), cur)]
    in_specs += [_resident(p.shape) for p in (*mix_params, *ffn_params)]
    scratch = [
        pltpu.VMEM((2, MIX_ROWS, D_MODEL), F32),
        pltpu.VMEM((DV, KEY_W), F32),
        pltpu.VMEM((DV, KEY_W), F32),
        pltpu.VMEM((SUBLANES, KEY_W), F32),
        pltpu.VMEM((SUBLANES, KEY_W), F32),
        pltpu.VMEM((SUBLANES, LANES), F32),
        pltpu.VMEM((MIX_ROWS + SUBLANES, VAL_W), F32),
        pltpu.VMEM((MIX_ROWS, KEY_W), F32),
        pltpu.VMEM((MIX_ROWS, SMALL_W), F32),
        pltpu.VMEM((MIX_ROWS, SMALL_W), F32),
        pltpu.VMEM((MIX_ROWS, SMALL_W), F32),
        pltpu.VMEM((MIX_ROWS, 2 * KEY_W), F32),
        pltpu.VMEM((MIX_ROWS, VAL_W), F32),
        pltpu.VMEM((MIX_ROWS, VAL_W), F32),
    ]
    return pl.pallas_call(
        functools.partial(_mix_ffn_body, blocks_per_seq=seq // MIX_ROWS),
        grid=(n_blocks + 1,),
        in_specs=in_specs,
        out_specs=pl.BlockSpec((MIX_ROWS, D_MODEL), prev),
        out_shape=jax.ShapeDtypeStruct((t, D_MODEL), F32),
        scratch_shapes=scratch,
        compiler_params=pltpu.CompilerParams(
            dimension_semantics=("arbitrary",), vmem_limit_bytes=VMEM_LIMIT),
        name="mix_ffn",
    )(proj, h1, *mix_params, *ffn_params)


def _prep_body(*refs):
    n_plain = (len(refs) - 2) // 2
    srcs, win_src = refs[:n_plain], refs[n_plain]
    dsts, win_dst = refs[n_plain + 1:-1], refs[-1]
    for src, dst in zip(srcs, dsts):
        dst[...] = src[...].astype(dst.dtype)
    head_w = G0 + VAL_W
    body_w = SM0 - MU0
    gate0 = head_w + GLA_RANK + body_w
    rows = win_dst.shape[0]
    win_dst[:, 0:head_w] = win_src[0, :, 0:head_w].astype(BF16)
    win_dst[:, MU0:SM0] = win_src[0, :, head_w + GLA_RANK:gate0].astype(BF16)
    win_dst[:, SM0:PROJ_W] = jnp.concatenate(
        [win_src[0, :, head_w:head_w + GLA_RANK], win_src[0, :, gate0:gate0 + 2 * HEADS],
         jnp.zeros((rows, SMALL_W - GLA_RANK - 2 * HEADS), F32)], axis=1).astype(BF16)


def _prep_weights(plain, w_in):
    def spec(rows, cols):
        assert rows % (BF16_ROWS * CAST_STEPS) == 0
        return pl.BlockSpec((rows // CAST_STEPS, cols), lambda i: (i, 0))
    _, rows, in_cols = w_in.shape
    win_spec = pl.BlockSpec((1, rows // CAST_STEPS, in_cols), lambda i: (0, i, 0))
    return pl.pallas_call(
        _prep_body,
        grid=(CAST_STEPS,),
        in_specs=[spec(*a.shape) for a in plain] + [win_spec],
        out_specs=[spec(*a.shape) for a in plain] + [spec(rows, PROJ_W)],
        out_shape=[jax.ShapeDtypeStruct(a.shape, BF16) for a in plain]
        + [jax.ShapeDtypeStruct((rows, PROJ_W), BF16)],
        compiler_params=pltpu.CompilerParams(
            dimension_semantics=("arbitrary",), vmem_limit_bytes=VMEM_LIMIT),
        name="prep_weights",
    )(*plain, w_in)


def _block_diag(w):
    h, dv, dk = w.shape
    eye = jnp.eye(h, dtype=w.dtype)
    return (w[:, :, None, :] * eye[:, None, :, None]).reshape(h * dv, h * dk)


def kernel(x, ln1_g, ln1_b, ffn1_w_gate, ffn1_w_up, ffn1_w_down, w_in, gla_w_lr, gla_b_lr,
           gla_norm_g, mlstm_conv_w, mlstm_conv_b, mlstm_w_q, mlstm_w_k, mlstm_b_i, mlstm_b_f,
           mlstm_norm_g, w_out, ln2_g, ln2_b, ffn2_w_gate, ffn2_w_up, ffn2_w_down, ln3_g, ln3_b):
    batch, seq, d = x.shape
    assert d == D_MODEL and seq % MIX_ROWS == 0 and (batch * seq) % FFN_PROJ_ROWS == 0
    assert ln1_g.shape[0] == 1, "single-layer problem"
    row = lambda v: v.reshape(1, -1).astype(F32)

    w1g, w1u, w1d, w2g, w2u, w2d, wo, win = _prep_weights(
        [ffn1_w_gate[0], ffn1_w_up[0], ffn1_w_down[0], ffn2_w_gate[0], ffn2_w_up[0],
         ffn2_w_down[0], w_out[0]], w_in)

    wlr = jnp.zeros((SMALL_W, KEY_W), F32).at[LR_LANE:LR_LANE + GLA_RANK].set(gla_w_lr[0])
    gbias = (jnp.zeros((1, SMALL_W), F32)
             .at[0, I_LANE:I_LANE + HEADS].set(mlstm_b_i[0])
             .at[0, F_LANE:F_LANE + HEADS].set(mlstm_b_f[0]))
    wqk = jnp.stack([
        jnp.concatenate([_block_diag(mlstm_w_q[0, 2 * p:2 * p + 2]),
                         _block_diag(mlstm_w_k[0, 2 * p:2 * p + 2])], axis=1)
        for p in range(PAIRS)])

    xf = x.reshape(batch * seq, d)
    h1, proj = _ffn_proj_call(xf, w1g, w1u, w1d, row(ln1_g), row(ln1_b), win)
    mix_params = [wlr.astype(BF16), row(gla_b_lr), row(gla_norm_g), mlstm_conv_w[0].astype(F32),
                  row(mlstm_conv_b), wqk.astype(BF16), gbias, row(mlstm_norm_g),
                  wo, row(ln2_g), row(ln2_b)]
    ffn_params = [w2g, w2u, w2d, row(ln3_g), row(ln3_b)]
    h3 = _mix_ffn_call(proj, h1, mix_params, ffn_params, seq)
    return h3.reshape(batch, seq, d)
```
